```python
import math
import jax
import jax.numpy as jnp
from jax import lax
import numpy as np

D_MODEL = 4096
BATCH = 4
SEQ = 2048
DEPTH = 4
DEC_BATCH = 32
DEC_SEQ = 1
PAST_LEN = 8192
PAGE_SIZE = 128

S5_WIDTH = D_MODEL // 4
S5_GROUP = 16
S5_GROUPS = S5_WIDTH // S5_GROUP
S5_STATE = 64
S5_DT_MIN = 1e-3
S5_DT_MAX = 1e-1
ATTN_HEAD_DIM = 64
ATTN_HEADS = D_MODEL // 256
ATTN_KV_HEADS = 4
ATTN_REP = ATTN_HEADS // ATTN_KV_HEADS
ATTN_WIDTH = ATTN_HEADS * ATTN_HEAD_DIM
KV_WIDTH = ATTN_KV_HEADS * ATTN_HEAD_DIM
WINDOW = 128
REL_BUCKETS = 32
REL_MAX_DIST = 128
SSD_WIDTH = D_MODEL // 2
SSD_HEAD_DIM = 64
SSD_HEADS = SSD_WIDTH // SSD_HEAD_DIM
SSD_GROUPS = 4
SSD_REP = SSD_HEADS // SSD_GROUPS
SSD_STATE = 128
CONV_WIDTH = 4
CONV_CH = SSD_WIDTH + 2 * SSD_GROUPS * SSD_STATE
SSD_CHUNK = 128
SSD_DT_MIN = 1e-3
SSD_DT_MAX = 1e-1
D_FF = 256 * ((8 * D_MODEL // 3 + 255) // 256)
N_BRANCH = 3
N_SUB = 3
MIX_WIDTH = S5_WIDTH + ATTN_WIDTH + SSD_WIDTH
IN_WIDTH = S5_WIDTH + ATTN_WIDTH + 2 * KV_WIDTH + SSD_WIDTH + CONV_CH + SSD_HEADS + N_BRANCH * D_MODEL
DEEPNORM_ALPHA = (2 * DEPTH) ** 0.25
DEEPNORM_BETA = (8 * DEPTH) ** -0.25
LN_EPS = 1e-5
RMS_EPS = 1e-5

kernel_name = 'hybrid_s5_swa_ssd_macaron_decode_step'


def _layer_norm(x, g, b):
    xf = x.astype(jnp.float32)
    mu = jnp.mean(xf, axis=-1, keepdims=True)
    var = jnp.mean(jnp.square(xf - mu), axis=-1, keepdims=True)
    return ((xf - mu) * lax.rsqrt(var + LN_EPS)).astype(x.dtype) * g + b


def _adaln(c, w, b):
    mod = (jax.nn.silu(c) @ w + b)[:, None, :]
    return jnp.split(mod, 3, axis=-1)


def _deepnorm_residual(x, out, gate, g, b):
    return _layer_norm(DEEPNORM_ALPHA * x + (1.0 + gate) * out, g, b)


def _swiglu(h, w1, w3, w2):
    return (jax.nn.silu(h @ w1) * (h @ w3)) @ w2


def _s5_mixer(u, lam_re, lam_im, log_dt, b_re, b_im, c_re, c_im, d_skip, w_glu, h0):
    bsz, seq = u.shape[0], u.shape[1]
    f32 = jnp.float32
    lam = lax.complex(lam_re.astype(f32), lam_im.astype(f32))
    dt = jnp.exp(log_dt.astype(f32))[:, None]
    a_bar = jnp.exp(lam * dt)
    b_mat = lax.complex(b_re.astype(f32), b_im.astype(f32))
    b_bar = ((a_bar - 1.0) / lam)[..., None] * b_mat
    c_mat = lax.complex(c_re.astype(f32), c_im.astype(f32))
    uf = u.astype(f32)
    ug = uf.reshape(bsz, seq, S5_GROUPS, S5_GROUP).astype(jnp.complex64)
    bu = jnp.einsum('gpi,blgi->blgp', b_bar, ug)
    a_seq = jnp.broadcast_to(a_bar, bu.shape)

    def combine(e1, e2):
        a1, x1 = e1
        a2, x2 = e2
        return a1 * a2, a2 * x1 + x2

    a_cum, h = lax.associative_scan(combine, (a_seq, bu), axis=1)
    if h0 is not None:
        h = h + a_cum * h0[:, None]
    y = jnp.einsum('gip,blgp->blgi', c_mat, h).real.reshape(bsz, seq, S5_WIDTH)
    y = jax.nn.gelu((y + d_skip.astype(f32) * uf).astype(u.dtype))
    val, gt = jnp.split(y @ w_glu, 2, axis=-1)
    return val * jax.nn.sigmoid(gt), h[:, -1]


def _rel_bucket(dist):
    max_exact = REL_BUCKETS // 2
    d = jnp.maximum(dist, 1).astype(jnp.float32)
    large = max_exact + (jnp.log(d / max_exact) / math.log(REL_MAX_DIST / max_exact)
                         * (REL_BUCKETS - max_exact)).astype(jnp.int32)
    large = jnp.minimum(large, REL_BUCKETS - 1)
    return jnp.where(dist < max_exact, dist, large)


def _rel_bias_heads(dist, rel_bias):
    bias = rel_bias[_rel_bucket(jnp.clip(dist, 0, WINDOW - 1))]
    qn, kn = dist.shape
    return jnp.transpose(bias, (2, 0, 1)).reshape(ATTN_KV_HEADS, ATTN_REP, qn, kn).astype(jnp.float32)


def _sink_softmax(logits, mask, sink):
    logits = jnp.where(mask, logits, -jnp.inf)
    sink_col = jnp.broadcast_to(sink.astype(jnp.float32).reshape(ATTN_KV_HEADS, ATTN_REP, 1, 1),
                                logits.shape[:-1] + (1,))
    p = jax.nn.softmax(jnp.concatenate([logits, sink_col], axis=-1), axis=-1)
    return p[..., :-1]


def _swa_prompt(q, k, v, rel_bias, sink):
    bsz, seq = q.shape[0], q.shape[1]
    nb = seq // WINDOW
    q = q.reshape(bsz, nb, WINDOW, ATTN_KV_HEADS, ATTN_REP, ATTN_HEAD_DIM)
    k = k.reshape(bsz, nb, WINDOW, ATTN_KV_HEADS, ATTN_HEAD_DIM)
    v = v.reshape(bsz, nb, WINDOW, ATTN_KV_HEADS, ATTN_HEAD_DIM)

    def with_prev(t):
        prev = jnp.concatenate([jnp.zeros_like(t[:, :1]), t[:, :-1]], axis=1)
        return jnp.concatenate([prev, t], axis=2)

    kb, vb = with_prev(k), with_prev(v)
    logits = jnp.einsum('bnqhrd,bnkhd->bnhrqk', q, kb).astype(jnp.float32) * ATTN_HEAD_DIM ** -0.5
    qi = jnp.arange(WINDOW)[:, None] + WINDOW
    kj = jnp.arange(2 * WINDOW)[None, :]
    dist = qi - kj
    in_win = (dist >= 0) & (dist < WINDOW)
    has_prev = (jnp.arange(nb) > 0)[:, None, None] | (kj >= WINDOW)[None]
    mask = (in_win[None] & has_prev)[None, :, None, None]
    p = _sink_softmax(logits + _rel_bias_heads(dist, rel_bias), mask, sink)
    o = jnp.einsum('bnhrqk,bnkhd->bnqhrd', p.astype(vb.dtype), vb).reshape(bsz, seq, ATTN_WIDTH)
    k_last = k.reshape(bsz, seq, ATTN_KV_HEADS, ATTN_HEAD_DIM)[:, -WINDOW:]
    v_last = v.reshape(bsz, seq, ATTN_KV_HEADS, ATTN_HEAD_DIM)[:, -WINDOW:]
    return o, k_last, v_last


def _swa_sample(q, k_new, v_new, k_win, v_win, rel_bias, sink):
    bsz, s = q.shape[0], q.shape[1]
    q = q.reshape(bsz, s, ATTN_KV_HEADS, ATTN_REP, ATTN_HEAD_DIM)
    k = jnp.concatenate([k_win.astype(k_new.dtype), k_new.reshape(bsz, s, ATTN_KV_HEADS, ATTN_HEAD_DIM)], axis=1)
    v = jnp.concatenate([v_win.astype(v_new.dtype), v_new.reshape(bsz, s, ATTN_KV_HEADS, ATTN_HEAD_DIM)], axis=1)
    logits = jnp.einsum('bqhrd,bkhd->bhrqk', q, k).astype(jnp.float32) * ATTN_HEAD_DIM ** -0.5
    dist = (WINDOW + jnp.arange(s))[:, None] - jnp.arange(WINDOW + s)[None, :]
    mask = (dist >= 0) & (dist < WINDOW)
    p = _sink_softmax(logits + _rel_bias_heads(dist, rel_bias), mask, sink)
    o = jnp.einsum('bhrqk,bkhd->bqhrd', p.astype(v.dtype), v).reshape(bsz, s, ATTN_WIDTH)
    return o, k[:, -WINDOW:], v[:, -WINDOW:]


def _segsum_exp(a):
    t = a.shape[-1]
    x = jnp.broadcast_to(a[..., None], a.shape + (t,))
    x = jnp.where(jnp.tril(jnp.ones((t, t), bool), -1), x, 0.0)
    s = jnp.cumsum(x, axis=-2)
    return jnp.exp(jnp.where(jnp.tril(jnp.ones((t, t), bool)), s, -jnp.inf))


def _ssd(x, dt, a, b, c, h0):
    bsz, seq = x.shape[0], x.shape[1]
    q_len = SSD_CHUNK if seq >= SSD_CHUNK else seq
    pad = (-seq) % q_len
    if pad:
        def padw(t):
            return jnp.pad(t, [(0, 0), (0, pad)] + [(0, 0)] * (t.ndim - 2))
        x, dt, b, c = padw(x), padw(dt), padw(b), padw(c)
    nc = (seq + pad) // q_len
    x = x.reshape(bsz, nc, q_len, SSD_GROUPS, SSD_REP, SSD_HEAD_DIM)
    dt = dt.reshape(bsz, nc, q_len, SSD_GROUPS, SSD_REP)
    b = b.reshape(bsz, nc, q_len, SSD_GROUPS, SSD_STATE)
    c = c.reshape(bsz, nc, q_len, SSD_GROUPS, SSD_STATE)
    adt = jnp.transpose(dt * a.reshape(SSD_GROUPS, SSD_REP), (0, 3, 4, 1, 2))
    xdt = x * dt[..., None]
    a_cs = jnp.cumsum(adt, axis=-1)
    decay = _segsum_exp(adt)
    cb = jnp.einsum('bclgn,bcsgn->bcgls', c, b)
    y_diag = jnp.einsum('bcgls,bgrcls,bcsgrp->bclgrp', cb, decay, xdt)
    decay_states = jnp.exp(a_cs[..., -1:] - a_cs)
    states = jnp.einsum('bcsgn,bgrcs,bcsgrp->bcgrpn', b, decay_states, xdt)
    if h0 is None:
        init = jnp.zeros((bsz, SSD_GROUPS, SSD_REP, SSD_HEAD_DIM, SSD_STATE), jnp.float32)
    else:
        init = h0.reshape(bsz, SSD_GROUPS, SSD_REP, SSD_HEAD_DIM, SSD_STATE)
    states = jnp.concatenate([init[:, None], states], axis=1)
    chunk_decay = _segsum_exp(jnp.pad(a_cs[..., -1], [(0, 0), (0, 0), (0, 0), (1, 0)]))
    new_states = jnp.einsum('bgrzc,bcgrpn->bzgrpn', chunk_decay, states)
    y_off = jnp.einsum('bclgn,bcgrpn,bgrcl->bclgrp', c, new_states[:, :-1], jnp.exp(a_cs))
    y = (y_diag + y_off).reshape(bsz, nc * q_len, SSD_HEADS, SSD_HEAD_DIM)[:, :seq]
    return y, new_states[:, -1].reshape(bsz, SSD_HEADS, SSD_HEAD_DIM, SSD_STATE)


def _ssd_mixer(z, xbc, dt_raw, conv_w, conv_b, dt_bias, a_log, d_skip, norm_w, conv0, h0):
    bsz, seq = xbc.shape[0], xbc.shape[1]
    if conv0 is None:
        buf = jnp.zeros((bsz, CONV_WIDTH - 1, CONV_CH), xbc.dtype)
    else:
        buf = conv0.astype(xbc.dtype)
    xp = jnp.concatenate([buf, xbc], axis=1)
    conv = lax.conv_general_dilated(xp, conv_w[:, None, :].astype(xp.dtype), window_strides=(1,),
                                    padding='VALID', dimension_numbers=('NWC', 'WIO', 'NWC'),
                                    feature_group_count=CONV_CH) + conv_b
    xbc_c = jax.nn.silu(conv)
    xs, bm, cm = jnp.split(xbc_c, [SSD_WIDTH, SSD_WIDTH + SSD_GROUPS * SSD_STATE], axis=-1)
    f32 = jnp.float32
    dt = jax.nn.softplus((dt_raw + dt_bias).astype(f32))
    a = -jnp.exp(a_log.astype(f32))
    x4 = xs.astype(f32).reshape(bsz, seq, SSD_HEADS, SSD_HEAD_DIM)
    y, h = _ssd(x4, dt, a,
                bm.astype(f32).reshape(bsz, seq, SSD_GROUPS, SSD_STATE),
                cm.astype(f32).reshape(bsz, seq, SSD_GROUPS, SSD_STATE), h0)
    y = (y + d_skip.astype(f32)[:, None] * x4).reshape(bsz, seq, SSD_WIDTH)
    y = y * jax.nn.silu(z.astype(f32))
    y = y * lax.rsqrt(jnp.mean(y * y, axis=-1, keepdims=True) + RMS_EPS)
    return y.astype(z.dtype) * norm_w, h, xp[:, -(CONV_WIDTH - 1):]


def _run_trunk(x, c, s5_re0, s5_im0, ssm0, conv0, k_win0, v_win0, weights):
    (w_ada, b_ada, ln_g, ln_b, ffn_w1, ffn_w3, ffn_w2, w_in, w_branch, w_out,
     s5_lam_re, s5_lam_im, s5_log_dt, s5_b_re, s5_b_im, s5_c_re, s5_c_im, s5_d, s5_w_glu,
     attn_sink, rel_bias, conv_w, conv_b, dt_bias, a_log, ssd_d, ssd_norm_w) = weights
    prompt = s5_re0 is None
    split_at = np.cumsum([S5_WIDTH, ATTN_WIDTH, KV_WIDTH, KV_WIDTH, SSD_WIDTH, CONV_CH, SSD_HEADS]).tolist()
    s5_re_l, s5_im_l, ssm_l, conv_l, k_l, v_l = [], [], [], [], [], []
    for l in range(DEPTH):
        shift, scale, gate = _adaln(c, w_ada[l, 0], b_ada[l, 0])
        h = x * (1.0 + scale) + shift
        x = _deepnorm_residual(x, 0.5 * _swiglu(h, ffn_w1[l, 0], ffn_w3[l, 0], ffn_w2[l, 0]),
                               gate, ln_g[l, 0], ln_b[l, 0])
        shift, scale, gate = _adaln(c, w_ada[l, 1], b_ada[l, 1])
        h = x * (1.0 + scale) + shift
        u, q, k, v, z, xbc, dt_raw, gates = jnp.split(h @ w_in[l], split_at, axis=-1)
        if prompt:
            s5_h0, ssm_h0, cbuf = None, None, None
        else:
            s5_h0 = lax.complex(s5_re0[l].astype(jnp.float32), s5_im0[l].astype(jnp.float32))
            ssm_h0 = ssm0[l].astype(jnp.float32)
            cbuf = conv0[l]
        o_a, s5_h = _s5_mixer(u, s5_lam_re[l], s5_lam_im[l], s5_log_dt[l], s5_b_re[l], s5_b_im[l],
                              s5_c_re[l], s5_c_im[l], s5_d[l], s5_w_glu[l], s5_h0)
        if prompt:
            o_b, k_keep, v_keep = _swa_prompt(q, k, v, rel_bias, attn_sink[l])
        else:
            o_b, k_keep, v_keep = _swa_sample(q, k, v, k_win0[l], v_win0[l], rel_bias, attn_sink[l])
        o_c, ssm_h, conv_keep = _ssd_mixer(z, xbc, dt_raw, conv_w[l], conv_b[l], dt_bias[l], a_log[l],
                                           ssd_d[l], ssd_norm_w[l], cbuf, ssm_h0)
        g_a, g_b, g_c = jnp.split(jax.nn.sigmoid(gates), N_BRANCH, axis=-1)
        wb = w_branch[l]
        merged = (g_a * (o_a @ wb[:S5_WIDTH])
                  + g_b * (o_b @ wb[S5_WIDTH:S5_WIDTH + ATTN_WIDTH])
                  + g_c * (o_c @ wb[S5_WIDTH + ATTN_WIDTH:]))
        x = _deepnorm_residual(x, merged @ w_out[l], gate, ln_g[l, 1], ln_b[l, 1])
        shift, scale, gate = _adaln(c, w_ada[l, 2], b_ada[l, 2])
        h = x * (1.0 + scale) + shift
        x = _deepnorm_residual(x, 0.5 * _swiglu(h, ffn_w1[l, 1], ffn_w3[l, 1], ffn_w2[l, 1]),
                               gate, ln_g[l, 2], ln_b[l, 2])
        s5_re_l.append(s5_h.real.astype(x.dtype))
        s5_im_l.append(s5_h.imag.astype(x.dtype))
        ssm_l.append(ssm_h.astype(x.dtype))
        conv_l.append(conv_keep.astype(x.dtype))
        k_l.append(k_keep.astype(x.dtype))
        v_l.append(v_keep.astype(x.dtype))
    return (x, jnp.stack(s5_re_l), jnp.stack(s5_im_l), jnp.stack(ssm_l),
            jnp.stack(conv_l), jnp.stack(k_l), jnp.stack(v_l))


def setup_inputs(seed: int = 0) -> dict:
    key = jax.random.key(seed)
    keys = iter(jax.random.split(key, 64))
    f32 = jnp.float32

    def normal(shape, scale):
        return jax.random.normal(next(keys), shape, f32) * scale

    def uniform(shape, lo, hi):
        return jax.random.uniform(next(keys), shape, f32, lo, hi)

    x_prompt = normal((BATCH, SEQ, D_MODEL), 1.0)
    x_sample = normal((DEC_BATCH, DEC_SEQ, D_MODEL), 1.0)
    state_s5_re = normal((DEPTH, DEC_BATCH, S5_GROUPS, S5_STATE), 0.1)
    state_s5_im = normal((DEPTH, DEC_BATCH, S5_GROUPS, S5_STATE), 0.1)
    state_ssm = normal((DEPTH, DEC_BATCH, SSD_HEADS, SSD_HEAD_DIM, SSD_STATE), 0.05)
    state_conv = normal((DEPTH, DEC_BATCH, CONV_WIDTH - 1, CONV_CH), 1.0)
    cache_k_win = normal((DEPTH, DEC_BATCH, WINDOW, ATTN_KV_HEADS, ATTN_HEAD_DIM), 1.0)
    cache_v_win = normal((DEPTH, DEC_BATCH, WINDOW, ATTN_KV_HEADS, ATTN_HEAD_DIM), 1.0)
    c_prompt = normal((BATCH, D_MODEL), 1.0)
    c_sample = normal((DEC_BATCH, D_MODEL), 1.0)

    w_ada = normal((DEPTH, N_SUB, D_MODEL, 3 * D_MODEL), 0.1 * D_MODEL ** -0.5)
    b_ada = normal((DEPTH, N_SUB, 3 * D_MODEL), 0.01)
    ln_g = 1.0 + normal((DEPTH, N_SUB, D_MODEL), 0.01)
    ln_b = normal((DEPTH, N_SUB, D_MODEL), 0.01)
    ffn_w1 = normal((DEPTH, 2, D_MODEL, D_FF), D_MODEL ** -0.5)
    ffn_w3 = normal((DEPTH, 2, D_MODEL, D_FF), D_MODEL ** -0.5)
    ffn_w2 = normal((DEPTH, 2, D_FF, D_MODEL), DEEPNORM_BETA * D_FF ** -0.5)
    w_in = normal((DEPTH, D_MODEL, IN_WIDTH), D_MODEL ** -0.5)
    w_branch = jnp.concatenate([normal((DEPTH, S5_WIDTH, D_MODEL), S5_WIDTH ** -0.5),
                                normal((DEPTH, ATTN_WIDTH, D_MODEL), ATTN_WIDTH ** -0.5),
                                normal((DEPTH, SSD_WIDTH, D_MODEL), SSD_WIDTH ** -0.5)], axis=1)
    w_out = normal((DEPTH, D_MODEL, D_MODEL), DEEPNORM_BETA * D_MODEL ** -0.5)

    s5_lam_re = -0.5 + normal((DEPTH, S5_GROUPS, S5_STATE), 0.01)
    s5_lam_im = math.pi * jnp.arange(S5_STATE, dtype=f32) + normal((DEPTH, S5_GROUPS, S5_STATE), 0.01)
    s5_log_dt = uniform((DEPTH, S5_GROUPS), math.log(S5_DT_MIN), math.log(S5_DT_MAX))
    s5_b_re = normal((DEPTH, S5_GROUPS, S5_STATE, S5_GROUP), (2 * S5_GROUP) ** -0.5)
    s5_b_im = normal((DEPTH, S5_GROUPS, S5_STATE, S5_GROUP), (2 * S5_GROUP) ** -0.5)
    s5_c_re = normal((DEPTH, S5_GROUPS, S5_GROUP, S5_STATE), S5_STATE ** -0.5)
    s5_c_im = normal((DEPTH, S5_GROUPS, S5_GROUP, S5_STATE), S5_STATE ** -0.5)
    s5_d = normal((DEPTH, S5_WIDTH), 1.0)
    s5_w_glu = normal((DEPTH, S5_WIDTH, 2 * S5_WIDTH), S5_WIDTH ** -0.5)

    attn_sink = normal((DEPTH, ATTN_HEADS), 0.5)
    rel_bias = normal((REL_BUCKETS, ATTN_HEADS), 0.5)

    conv_w = normal((DEPTH, CONV_WIDTH, CONV_CH), CONV_WIDTH ** -0.5)
    conv_b = normal((DEPTH, CONV_CH), 0.01)
    dt0 = jnp.exp(uniform((DEPTH, SSD_HEADS), math.log(SSD_DT_MIN), math.log(SSD_DT_MAX)))
    dt_bias = dt0 + jnp.log(-jnp.expm1(-dt0))
    a_log = jnp.log(uniform((DEPTH, SSD_HEADS), 1.0, 16.0))
    ssd_d = 1.0 + normal((DEPTH, SSD_HEADS), 0.01)
    ssd_norm_w = 1.0 + normal((DEPTH, SSD_WIDTH), 0.01)

    return {'x_prompt': x_prompt, 'x_sample': x_sample,
            'state_s5_re': state_s5_re, 'state_s5_im': state_s5_im, 'state_ssm': state_ssm,
            'state_conv': state_conv, 'cache_k_win': cache_k_win, 'cache_v_win': cache_v_win,
            'c_prompt': c_prompt, 'c_sample': c_sample,
            'w_ada': w_ada, 'b_ada': b_ada, 'ln_g': ln_g, 'ln_b': ln_b,
            'ffn_w1': ffn_w1, 'ffn_w3': ffn_w3, 'ffn_w2': ffn_w2,
            'w_in': w_in, 'w_branch': w_branch, 'w_out': w_out,
            's5_lam_re': s5_lam_re, 's5_lam_im': s5_lam_im, 's5_log_dt': s5_log_dt,
            's5_b_re': s5_b_re, 's5_b_im': s5_b_im, 's5_c_re': s5_c_re, 's5_c_im': s5_c_im,
            's5_d': s5_d, 's5_w_glu': s5_w_glu,
            'attn_sink': attn_sink, 'rel_bias': rel_bias,
            'conv_w': conv_w, 'conv_b': conv_b, 'dt_bias': dt_bias, 'a_log': a_log,
            'ssd_d': ssd_d, 'ssd_norm_w': ssd_norm_w}


def reference(x_prompt, x_sample, state_s5_re, state_s5_im, state_ssm, state_conv, cache_k_win, cache_v_win,
              c_prompt, c_sample, w_ada, b_ada, ln_g, ln_b, ffn_w1, ffn_w3, ffn_w2, w_in, w_branch, w_out,
              s5_lam_re, s5_lam_im, s5_log_dt, s5_b_re, s5_b_im, s5_c_re, s5_c_im, s5_d, s5_w_glu,
              attn_sink, rel_bias, conv_w, conv_b, dt_bias, a_log, ssd_d, ssd_norm_w):
    weights = (w_ada, b_ada, ln_g, ln_b, ffn_w1, ffn_w3, ffn_w2, w_in, w_branch, w_out,
               s5_lam_re, s5_lam_im, s5_log_dt, s5_b_re, s5_b_im, s5_c_re, s5_c_im, s5_d, s5_w_glu,
               attn_sink, rel_bias, conv_w, conv_b, dt_bias, a_log, ssd_d, ssd_norm_w)
    y_prompt, p_s5_re, p_s5_im, p_ssm, p_conv, p_k_win, p_v_win = _run_trunk(
        x_prompt, c_prompt, None, None, None, None, None, None, weights)
    y_sample, s_s5_re, s_s5_im, s_ssm, s_conv, s_k_win, s_v_win = _run_trunk(
        x_sample, c_sample, state_s5_re, state_s5_im, state_ssm, state_conv, cache_k_win, cache_v_win, weights)
    return (y_prompt, y_sample, p_s5_re, p_s5_im, p_ssm, p_conv, p_k_win, p_v_win,
            s_s5_re, s_s5_im, s_ssm, s_conv, s_k_win, s_v_win)
```

```python
import functools
import math

import jax
import jax.numpy as jnp
from jax import lax
from jax.experimental import pallas as pl
from jax.experimental.pallas import tpu as pltpu

F32 = jnp.float32
BF16 = jnp.bfloat16

LN_EPS = 1e-5
RMS_EPS = 1e-5
SSD_CHUNK = 128
S5_CHUNK = 16
LANES = 128
SUBLANES = 8
VMEM_BYTES_V7X = 64 * 1024 * 1024
VMEM_LIMIT = VMEM_BYTES_V7X - 8 * 1024 * 1024
NEG_BIG = -1e30

_NT = (((1,), (1,)), ((), ()))
_TN = (((0,), (0,)), ((), ()))


def _pick_tile(n, cap, align):
    if n <= cap:
        return n
    t = (cap // align) * align
    while t >= align:
        if n % t == 0:
            return t
        t -= align
    raise ValueError(f"no tile for {n} under {cap} aligned to {align}")


def _cparams(*sem):
    return pltpu.CompilerParams(dimension_semantics=sem, vmem_limit_bytes=VMEM_LIMIT)


def _sigmoid(x):
    return 1.0 / (1.0 + jnp.exp(-x))


def _silu(x):
    return x * _sigmoid(x)


def _softplus(x):
    return jnp.maximum(x, 0.0) + jnp.log1p(jnp.exp(-jnp.abs(x)))


def _dot(a, b, precision=None):
    return jnp.dot(a, b, preferred_element_type=F32, precision=precision)


def _adaln_kernel(c_ref, w_ref, b_ref, o_ref):
    a = _silu(c_ref[...]).astype(BF16)
    o_ref[...] = _dot(a, w_ref[...].astype(BF16)) + b_ref[...]


def _adaln(c_all, w_ada, b_ada):
    n_ls, d, n3 = w_ada.shape
    rows = c_all.shape[0]
    tn = _pick_tile(n3, 512, LANES)
    return pl.pallas_call(
        _adaln_kernel,
        grid=(n_ls, n3 // tn),
        in_specs=[pl.BlockSpec((rows, d), lambda s, j: (0, 0)),
                  pl.BlockSpec((None, d, tn), lambda s, j: (s, 0, j)),
                  pl.BlockSpec((None, 1, tn), lambda s, j: (s, 0, j))],
        out_specs=pl.BlockSpec((None, rows, tn), lambda s, j: (s, 0, j)),
        out_shape=jax.ShapeDtypeStruct((n_ls, rows, n3), F32),
        compiler_params=_cparams("arbitrary", "arbitrary"),
    )(c_all, w_ada, b_ada)


class _Mods:
    def __init__(self, mods, prompt, dec_b, seq, d):
        self.prompt, self.dec_b, self.seq, self.d = prompt, dec_b, seq, d
        n_ls, rows, n3 = mods.shape
        self.arr = mods.reshape(n_ls, rows, 1, n3) if prompt else mods

    def spec(self, ls, part, tm):
        d = self.d
        if self.prompt:
            dec_b, seq = self.dec_b, self.seq
            return pl.BlockSpec((None, None, 1, d), lambda i, *_: (ls, dec_b + (i * tm) // seq, 0, part))
        return pl.BlockSpec((None, self.dec_b, d), lambda i, *_: (ls, 0, part))


def _modulate_kernel(x_ref, sh_ref, sc_ref, h_ref):
    h_ref[...] = (x_ref[...] * (1.0 + sc_ref[...]) + sh_ref[...]).astype(BF16)


def _modulate(x, mods, ls):
    m, d = x.shape
    tm = _pick_tile(mods.seq, 256, 16) if mods.prompt else m
    return pl.pallas_call(
        _modulate_kernel,
        grid=(m // tm,),
        in_specs=[pl.BlockSpec((tm, d), lambda i: (i, 0)), mods.spec(ls, 0, tm), mods.spec(ls, 1, tm)],
        out_specs=pl.BlockSpec((tm, d), lambda i: (i, 0)),
        out_shape=jax.ShapeDtypeStruct((m, d), BF16),
        compiler_params=_cparams("arbitrary"),
    )(x, mods.arr, mods.arr)


def _deepnorm_kernel(x_ref, y_ref, gate_ref, g_ref, b_ref, *rest, alpha, coef, with_h):
    t = alpha * x_ref[...] + (1.0 + gate_ref[...]) * (coef * y_ref[...])
    mu = jnp.mean(t, axis=-1, keepdims=True)
    dlt = t - mu
    var = jnp.mean(dlt * dlt, axis=-1, keepdims=True)
    xn = dlt * lax.rsqrt(var + LN_EPS) * g_ref[...] + b_ref[...]
    if with_h:
        sh_ref, sc_ref, xo_ref, ho_ref = rest
        ho_ref[...] = (xn * (1.0 + sc_ref[...]) + sh_ref[...]).astype(BF16)
    else:
        (xo_ref,) = rest
    xo_ref[...] = xn


def _deepnorm(x, y, mods, ls, ln_g, ln_b, next_ls, alpha, coef):
    m, d = x.shape
    tm = _pick_tile(mods.seq, 256, 16) if mods.prompt else m
    with_h = next_ls is not None
    row = pl.BlockSpec((tm, d), lambda i: (i, 0))
    vec = pl.BlockSpec((None, 1, d), lambda i: (ls, 0, 0))
    in_specs = [row, row, mods.spec(ls, 2, tm), vec, vec]
    args = [x, y, mods.arr, ln_g, ln_b]
    out_specs, out_shape = [row], [jax.ShapeDtypeStruct((m, d), F32)]
    if with_h:
        in_specs += [mods.spec(next_ls, 0, tm), mods.spec(next_ls, 1, tm)]
        args += [mods.arr, mods.arr]
        out_specs.append(row)
        out_shape.append(jax.ShapeDtypeStruct((m, d), BF16))
    res = pl.pallas_call(
        functools.partial(_deepnorm_kernel, alpha=alpha, coef=coef, with_h=with_h),
        grid=(m // tm,), in_specs=in_specs, out_specs=out_specs, out_shape=out_shape,
        compiler_params=_cparams("arbitrary"),
    )(*args)
    return (res[0], res[1]) if with_h else (res[0], None)


def _mm_kernel(x_ref, w_ref, o_ref, *, act):
    r = _dot(x_ref[...], w_ref[...])
    if act == "sigmoid":
        r = _sigmoid(r)
    o_ref[...] = r.astype(o_ref.dtype)


def _row_tile(m, k):
    cap = 1024 if k <= 4096 else 512
    return _pick_tile(m, cap, 16)


def _mm(x, w, widx, act=None, out_dtype=F32):
    m, k = x.shape
    n = w.shape[-1]
    tm, tn = _row_tile(m, k), _pick_tile(n, 512, LANES)
    return pl.pallas_call(
        functools.partial(_mm_kernel, act=act),
        grid=(m // tm, n // tn),
        in_specs=[pl.BlockSpec((tm, k), lambda i, j: (i, 0)),
                  pl.BlockSpec((None, k, tn), lambda i, j: (widx, 0, j))],
        out_specs=pl.BlockSpec((tm, tn), lambda i, j: (i, j)),
        out_shape=jax.ShapeDtypeStruct((m, n), out_dtype),
        compiler_params=_cparams("arbitrary", "arbitrary"),
    )(x, w)


def _ffn_up_kernel(h_ref, w1_ref, w3_ref, o_ref):
    h = h_ref[...]
    a = _dot(h, w1_ref[...])
    b = _dot(h, w3_ref[...])
    o_ref[...] = (_silu(a) * b).astype(BF16)


def _ffn_up(h, w1, w3, widx):
    m, k = h.shape
    n = w1.shape[-1]
    tm, tn = _row_tile(m, k), _pick_tile(n, 256, LANES)
    wspec = pl.BlockSpec((None, k, tn), lambda i, j: (widx, 0, j))
    return pl.pallas_call(
        _ffn_up_kernel,
        grid=(m // tm, n // tn),
        in_specs=[pl.BlockSpec((tm, k), lambda i, j: (i, 0)), wspec, wspec],
        out_specs=pl.BlockSpec((tm, tn), lambda i, j: (i, j)),
        out_shape=jax.ShapeDtypeStruct((m, n), BF16),
        compiler_params=_cparams("arbitrary", "arbitrary"),
    )(h, w1, w3)


def _merge_kernel(oa_ref, ob_ref, oc_ref, wa_ref, wb_ref, wc_ref, ga_ref, gb_ref, gc_ref, o_ref):
    o = (ga_ref[...] * _dot(oa_ref[...], wa_ref[...])
         + gb_ref[...] * _dot(ob_ref[...], wb_ref[...])
         + gc_ref[...] * _dot(oc_ref[...], wc_ref[...]))
    o_ref[...] = o.astype(BF16)


def _merge(o_a, o_b, o_c, w_branch, lyr, gates):
    m, ka = o_a.shape
    kb, kc = o_b.shape[1], o_c.shape[1]
    d = w_branch.shape[-1]
    assert ka == kb and kc % (ka + kb) == 0, "branch widths must tile the stacked branch weight"
    tm, tn = _pick_tile(m, 1024, 16), _pick_tile(d, 512, LANES)
    nj = d // tn
    rows = lambda width: pl.BlockSpec((tm, width), lambda i, j: (i, 0))
    return pl.pallas_call(
        _merge_kernel,
        grid=(m // tm, nj),
        in_specs=[rows(ka), rows(kb), rows(kc),
                  pl.BlockSpec((None, ka, tn), lambda i, j: (lyr, 0, j)),
                  pl.BlockSpec((None, kb, tn), lambda i, j: (lyr, 1, j)),
                  pl.BlockSpec((None, kc, tn), lambda i, j: (lyr, (ka + kb) // kc, j)),
                  pl.BlockSpec((tm, tn), lambda i, j: (i, j)),
                  pl.BlockSpec((tm, tn), lambda i, j: (i, nj + j)),
                  pl.BlockSpec((tm, tn), lambda i, j: (i, 2 * nj + j))],
        out_specs=pl.BlockSpec((tm, tn), lambda i, j: (i, j)),
        out_shape=jax.ShapeDtypeStruct((m, d), BF16),
        compiler_params=_cparams("arbitrary", "arbitrary"),
    )(o_a, o_b, o_c, w_branch, w_branch, w_branch, gates, gates, gates)


def _s5_kernel(u_ref, st_ref, tt_ref, rt_ref, a1_ref, a2_ref, h0_ref, y_ref, hf_ref, e_scr, hp_scr,
               *, n_chunks, nb):
    hi = lax.Precision.HIGHEST
    u = u_ref[...]
    e_scr[...] = _dot(u, st_ref[...], hi)
    a1, a2 = a1_ref[...], a2_ref[...]
    half = a1.shape[-1] // 2

    def body(c, h):
        r = pl.multiple_of(c * nb, nb)
        hp_scr[pl.ds(r, nb), :] = h
        return h * a1 + pltpu.roll(h, half, axis=1) * a2 + e_scr[pl.ds(r, nb), :]

    hf_ref[...] = lax.fori_loop(0, n_chunks, body, h0_ref[...])
    y_ref[...] = _dot(u, tt_ref[...], hi) + _dot(hp_scr[...], rt_ref[...], hi)


def _s5_scan(u_t, ops, h0, n_chunks, nb):
    st, tt, rt, a1, a2 = ops
    g, n, kq = u_t.shape
    p2 = st.shape[-1]
    blk = lambda *shape: pl.BlockSpec((None,) + shape, lambda i: (i,) + (0,) * len(shape))
    return pl.pallas_call(
        functools.partial(_s5_kernel, n_chunks=n_chunks, nb=nb),
        grid=(g,),
        in_specs=[blk(n, kq), blk(kq, p2), blk(kq, kq), blk(p2, kq), blk(1, p2), blk(1, p2), blk(nb, p2)],
        out_specs=[blk(n, kq), blk(nb, p2)],
        out_shape=[jax.ShapeDtypeStruct((g, n, kq), F32), jax.ShapeDtypeStruct((g, nb, p2), F32)],
        scratch_shapes=[pltpu.VMEM((n, p2), F32), pltpu.VMEM((n, p2), F32)],
        compiler_params=_cparams("arbitrary"),
    )(u_t, st, tt, rt, a1, a2, h0)


def _cmul(ar, ai, br, bi):
    return ar * br - ai * bi, ar * bi + ai * br


def _s5_operators(lam_re, lam_im, log_dt, b_re, b_im, c_re, c_im, q, kq_pad):
    hi = lax.Precision.HIGHEST
    g, p = lam_re.shape
    i_ch = b_re.shape[-1]
    dt = jnp.exp(log_dt)[:, None]
    t = jnp.arange(q + 1, dtype=F32)[:, None, None]
    mag = jnp.exp(lam_re[None] * dt[None] * t)
    ang = lam_im[None] * dt[None] * t
    ap_re, ap_im = mag * jnp.cos(ang), mag * jnp.sin(ang)
    nr, ni = ap_re[1] - 1.0, ap_im[1]
    den = lam_re * lam_re + lam_im * lam_im
    f_re, f_im = (nr * lam_re + ni * lam_im) / den, (ni * lam_re - nr * lam_im) / den
    bb_re, bb_im = _cmul(f_re[..., None], f_im[..., None], b_re, b_im)
    ab_re, ab_im = _cmul(ap_re[:q, :, :, None], ap_im[:q, :, :, None], bb_re[None], bb_im[None])
    k_t = (jnp.einsum("gip,tgpj->tgij", c_re, ab_re, precision=hi)
           - jnp.einsum("gip,tgpj->tgij", c_im, ab_im, precision=hi))
    l_i, s_i = jnp.arange(q)[:, None], jnp.arange(q)[None, :]
    tm = jnp.where((l_i >= s_i)[:, :, None, None, None], k_t[jnp.clip(l_i - s_i, 0, q - 1)], 0.0)
    tt = jnp.transpose(tm, (2, 1, 4, 0, 3)).reshape(g, q * i_ch, q * i_ch)
    rev = q - 1 - jnp.arange(q)
    sb_re, sb_im = _cmul(ap_re[rev][..., None], ap_im[rev][..., None], bb_re[None], bb_im[None])
    to_sj = lambda v: jnp.transpose(v, (1, 0, 3, 2)).reshape(g, q * i_ch, p)
    st = jnp.concatenate([to_sj(sb_re), to_sj(sb_im)], axis=-1)
    rc_re, rc_im = _cmul(c_re[None], c_im[None], ap_re[1:q + 1][:, :, None, :], ap_im[1:q + 1][:, :, None, :])
    to_li = lambda v: jnp.transpose(v, (1, 3, 0, 2)).reshape(g, p, q * i_ch)
    rt = jnp.concatenate([to_li(rc_re), -to_li(rc_im)], axis=1)
    a1 = jnp.concatenate([ap_re[q], ap_re[q]], axis=-1)[:, None, :]
    a2 = jnp.concatenate([-ap_im[q], ap_im[q]], axis=-1)[:, None, :]
    pad = kq_pad - q * i_ch
    if pad:
        st = jnp.pad(st, ((0, 0), (0, pad), (0, 0)))
        tt = jnp.pad(tt, ((0, 0), (0, pad), (0, pad)))
        rt = jnp.pad(rt, ((0, 0), (0, 0), (0, pad)))
    return st, tt, rt, a1, a2


def _s5_glu_kernel(y_ref, u_ref, d_ref, w_ref, o_ref):
    a = jax.nn.gelu(y_ref[...] + d_ref[...] * u_ref[...], approximate=True)
    r = _dot(a.astype(BF16), w_ref[...])
    wd = o_ref.shape[-1]
    o_ref[...] = (r[:, :wd] * _sigmoid(r[:, wd:])).astype(BF16)


def _s5_glu(y, u, d_skip, w_glu, lyr):
    m, wd = y.shape
    tm = _pick_tile(m, 512, 16)
    row = pl.BlockSpec((tm, wd), lambda i: (i, 0))
    return pl.pallas_call(
        _s5_glu_kernel,
        grid=(m // tm,),
        in_specs=[row, row, pl.BlockSpec((None, 1, wd), lambda i: (lyr, 0, 0)),
                  pl.BlockSpec((None, wd, 2 * wd), lambda i: (lyr, 0, 0))],
        out_specs=row,
        out_shape=jax.ShapeDtypeStruct((m, wd), BF16),
        compiler_params=_cparams("arbitrary"),
    )(y, u, d_skip, w_glu)


def _swa_kernel(sink_ref, q_ref, kp_ref, kc_ref, vp_ref, vc_ref, bias_ref, o_ref,
                *, n_kv, rep, hd, first_has_prev, scale):
    n = pl.program_id(1)
    wn = q_ref.shape[0]
    q = q_ref[...]
    k = jnp.concatenate([kp_ref[...], kc_ref[...]], axis=0).astype(BF16)
    v = jnp.concatenate([vp_ref[...], vc_ref[...]], axis=0).astype(BF16)
    l_i = lax.broadcasted_iota(jnp.int32, (rep, wn, 2 * wn), 1).reshape(rep * wn, 2 * wn)
    k_j = lax.broadcasted_iota(jnp.int32, (rep * wn, 2 * wn), 1)
    dist = l_i + wn - k_j
    valid = (dist >= 0) & (dist < wn)
    if not first_has_prev:
        valid = valid & ((n > 0) | (k_j >= wn))
    for h in range(n_kv):
        heads = [h * rep + r for r in range(rep)]
        qs = jnp.concatenate([q[:, a * hd:(a + 1) * hd] for a in heads], axis=0).astype(BF16)
        kh, vh = k[:, h * hd:(h + 1) * hd], v[:, h * hd:(h + 1) * hd]
        s = lax.dot_general(qs, kh, _NT, preferred_element_type=F32) * scale + bias_ref[h]
        s = jnp.where(valid, s, NEG_BIG)
        sink = jnp.concatenate([jnp.full((wn, 1), sink_ref[a], F32) for a in heads], axis=0)
        mx = jnp.maximum(jnp.max(s, axis=1, keepdims=True), sink)
        p = jnp.exp(s - mx)
        den = jnp.sum(p, axis=1, keepdims=True) + jnp.exp(sink - mx)
        o = _dot(p.astype(BF16), vh) / den
        for r, a in enumerate(heads):
            o_ref[:, a * hd:(a + 1) * hd] = o[r * wn:(r + 1) * wn].astype(BF16)


def _swa(q_arr, q_col, kp, kc, vp, vc, bias, sink, n_seq, n_blk, n_kv, hd, first_has_prev):
    n_heads = sink.shape[0]
    rep = n_heads // n_kv
    wn = bias.shape[1] // rep
    qw, kw = n_heads * hd, n_kv * hd

    def kv_spec(src):
        _, rmap, col = src
        return pl.BlockSpec((wn, kw), lambda b, n: (rmap(b, n), col))

    return pl.pallas_call(
        functools.partial(_swa_kernel, n_kv=n_kv, rep=rep, hd=hd, first_has_prev=first_has_prev,
                          scale=hd ** -0.5),
        grid=(n_seq, n_blk),
        in_specs=[pl.BlockSpec(memory_space=pltpu.SMEM),
                  pl.BlockSpec((wn, qw), lambda b, n: (b * n_blk + n, q_col)),
                  kv_spec(kp), kv_spec(kc), kv_spec(vp), kv_spec(vc),
                  pl.BlockSpec(bias.shape, lambda b, n: (0, 0, 0))],
        out_specs=pl.BlockSpec((wn, qw), lambda b, n: (b * n_blk + n, 0)),
        out_shape=jax.ShapeDtypeStruct((n_seq * n_blk * wn, qw), BF16),
        compiler_params=_cparams("arbitrary", "arbitrary"),
    )(sink, q_arr, kp[0], kc[0], vp[0], vc[0], bias)


def _rel_bias_table(rel_bias, wn, n_kv):
    n_buckets, n_heads = rel_bias.shape
    max_exact = n_buckets // 2
    dist = (jnp.arange(wn)[:, None] + wn) - jnp.arange(2 * wn)[None, :]
    dist = jnp.clip(dist, 0, wn - 1)
    d = jnp.maximum(dist, 1).astype(F32)
    large = max_exact + (jnp.log(d / max_exact) / math.log(wn / max_exact)
                         * (n_buckets - max_exact)).astype(jnp.int32)
    bucket = jnp.where(dist < max_exact, dist, jnp.minimum(large, n_buckets - 1))
    bias = jnp.transpose(rel_bias[bucket], (2, 0, 1)).astype(F32)
    return bias.reshape(n_kv, (n_heads // n_kv) * wn, 2 * wn)


def _ssd_kernel(xbc_ref, z_ref, dt_ref, dtt_ref, conv0_ref, h0_ref, cw_ref, cb_ref, dtb_ref, dtbt_ref,
                alog_ref, alogt_ref, dsk_ref, nw_ref, o_ref, hf_ref, xp_scr, st_scr, y_scr,
                *, n_heads, hp, n_groups, ns, n_valid):
    hi = lax.Precision.HIGHEST
    c = pl.program_id(1)
    q = xbc_ref.shape[0]
    width = n_heads * hp
    rep = n_heads // n_groups
    taps = cw_ref.shape[0]

    @pl.when(c == 0)
    def _():
        xp_scr[0:SUBLANES, :] = conv0_ref[...]
        st_scr[...] = h0_ref[...]

    xp_scr[SUBLANES:SUBLANES + q, :] = xbc_ref[...]
    conv = cb_ref[...]
    for w in range(taps):
        off = SUBLANES - (taps - 1) + w
        conv = conv + cw_ref[w:w + 1, :] * xp_scr[off:off + q, :]
    xp_scr[0:SUBLANES, :] = xp_scr[q:q + SUBLANES, :]
    xc = _silu(conv)
    xs = xc[:, :width]
    bm = xc[:, width:width + n_groups * ns]
    cm = xc[:, width + n_groups * ns:]

    row = lax.broadcasted_iota(jnp.int32, (q, q), 0)
    col = lax.broadcasted_iota(jnp.int32, (q, q), 1)
    tri = row >= col
    dt = _softplus(dt_ref[...] + dtb_ref[...])
    dtt = _softplus(dtt_ref[...] + dtbt_ref[...])
    if n_valid < q:
        dt = jnp.where(lax.broadcasted_iota(jnp.int32, dt.shape, 0) < n_valid, dt, 0.0)
        dtt = jnp.where(lax.broadcasted_iota(jnp.int32, dtt.shape, 1) < n_valid, dtt, 0.0)
    adt = dt * -jnp.exp(alog_ref[...])
    adtt = dtt * -jnp.exp(alogt_ref[...])
    cs = _dot(tri.astype(F32), adt, hi)
    cst = _dot(adtt, (row <= col).astype(F32), hi)
    cs_last = cs[q - 1:q, :]
    e_cs = jnp.exp(cs)
    dec = jnp.exp(cs_last - cs)
    e_last = jnp.exp(cs_last)

    for g in range(n_groups):
        bg = bm[:, g * ns:(g + 1) * ns]
        cg = cm[:, g * ns:(g + 1) * ns].astype(BF16)
        cb = lax.dot_general(cg, bg.astype(BF16), _NT, preferred_element_type=F32)
        for r in range(rep):
            h = g * rep + r
            x_h = xs[:, h * hp:(h + 1) * hp]
            xdt = (x_h * dt[:, h:h + 1]).astype(BF16)
            seg = cs[:, h:h + 1] - cst[h:h + 1, :]
            lm = jnp.exp(jnp.where(tri, seg, NEG_BIG))
            y = _dot((cb * lm).astype(BF16), xdt)
            s_prev = st_scr[h]
            y_off = lax.dot_general(cg, s_prev.astype(BF16), _NT, preferred_element_type=F32) * e_cs[:, h:h + 1]
            bw = (bg * dec[:, h:h + 1]).astype(BF16)
            st_scr[h] = e_last[:, h:h + 1] * s_prev + lax.dot_general(xdt, bw, _TN, preferred_element_type=F32)
            y_scr[:, h * hp:(h + 1) * hp] = y + y_off + dsk_ref[:, h * hp:(h + 1) * hp] * x_h

    yg = y_scr[...] * _silu(z_ref[...])
    o = yg * lax.rsqrt(jnp.mean(yg * yg, axis=-1, keepdims=True) + RMS_EPS) * nw_ref[...]
    o_ref[...] = o.astype(BF16)

    @pl.when(c == pl.num_programs(1) - 1)
    def _():
        hf_ref[...] = st_scr[...]


def _ssd(xbc, z, dt_raw, conv0, h0, params, lyr, n_seq, n_chunks, n_valid):
    conv_w, conv_b, dtb, dtbt, alog, alogt, dsk, nw, n_groups = params
    _, n_heads, hp, ns = h0.shape
    q = SSD_CHUNK
    ch, width = xbc.shape[1], z.shape[1]
    dtt = jnp.transpose(dt_raw[:, :n_heads].reshape(n_seq * n_chunks, q, n_heads), (0, 2, 1))
    rowblk = lambda wd: pl.BlockSpec((q, wd), lambda b, c: (b * n_chunks + c, 0))
    lyr_blk = lambda *shape: pl.BlockSpec((None,) + shape, lambda b, c: (lyr,) + (0,) * len(shape))
    seq_blk = lambda *shape: pl.BlockSpec((None,) + shape, lambda b, c: (b,) + (0,) * len(shape))
    return pl.pallas_call(
        functools.partial(_ssd_kernel, n_heads=n_heads, hp=hp, n_groups=n_groups, ns=ns, n_valid=n_valid),
        grid=(n_seq, n_chunks),
        in_specs=[rowblk(ch), rowblk(width), rowblk(LANES),
                  pl.BlockSpec((None, n_heads, q), lambda b, c: (b * n_chunks + c, 0, 0)),
                  seq_blk(SUBLANES, ch), seq_blk(n_heads, hp, ns),
                  lyr_blk(conv_w.shape[1], ch), lyr_blk(1, ch), lyr_blk(1, LANES), lyr_blk(n_heads, 1),
                  lyr_blk(1, LANES), lyr_blk(n_heads, 1), lyr_blk(1, width), lyr_blk(1, width)],
        out_specs=[rowblk(width), seq_blk(n_heads, hp, ns)],
        out_shape=[jax.ShapeDtypeStruct((n_seq * n_chunks * q, width), BF16),
                   jax.ShapeDtypeStruct((n_seq, n_heads, hp, ns), F32)],
        scratch_shapes=[pltpu.VMEM((q + SUBLANES, ch), F32), pltpu.VMEM((n_heads, hp, ns), F32),
                        pltpu.VMEM((q, width), F32)],
        compiler_params=_cparams("arbitrary", "arbitrary"),
    )(xbc, z, dt_raw, dtt, conv0, h0, conv_w, conv_b, dtb, dtbt, alog, alogt, dsk, nw)


def _pad_rows_per_seq(a, rows):
    n_seq, c = a.shape
    return jnp.pad(a[:, None, :], ((0, 0), (0, rows - 1), (0, 0))).reshape(n_seq * rows, c)


def _run_trunk(x, mods, wts, dims, states):
    (w1, w3, w2, w_u, w_qkv, w_z, w_xbc, w_dt, w_gates, w_branch, w_out, w_glu,
     ln_g, ln_b, s5_params, s5_d, attn_sink, bias_tab, ssd_params) = wts
    depth, n_seq, seq, d = dims["depth"], dims["n_seq"], dims["seq"], dims["d"]
    n_kv, hd, wn = dims["n_kv"], dims["hd"], dims["window"]
    g5, i5, p5 = dims["s5_groups"], dims["s5_group"], dims["s5_state"]
    n_heads_ssd, hp, ns, taps, ch = dims["ssd_heads"], dims["ssd_hp"], dims["ssd_state"], dims["taps"], dims["conv_ch"]
    prompt = states is None
    alpha = (2 * depth) ** 0.25
    kw = n_kv * hd
    qw = attn_sink.shape[1] * hd
    m = x.shape[0]

    outs = {k: [] for k in ("s5_re", "s5_im", "ssm", "conv", "k", "v")}
    h = _modulate(x, mods, 0)
    for lyr in range(depth):
        ls = 3 * lyr
        y = _mm(_ffn_up(h, w1, w3, 2 * lyr), w2, 2 * lyr)
        x, h = _deepnorm(x, y, mods, ls, ln_g, ln_b, ls + 1, alpha, 0.5)

        u = _mm(h, w_u, lyr)
        qkv = _mm(h, w_qkv, lyr)
        z = _mm(h, w_z, lyr)
        xbc = _mm(h, w_xbc, lyr)
        dt_raw = _mm(h, w_dt, lyr)
        gates = _mm(h, w_gates, lyr, act="sigmoid")

        if prompt:
            q5 = S5_CHUNK
            n_c = seq // q5
            nb = -(-n_seq // SUBLANES) * SUBLANES
            ops = _s5_operators(*s5_params(lyr), q5, q5 * i5)
            u_t = jnp.transpose(u.reshape(n_seq, n_c, q5, g5, i5), (3, 1, 0, 2, 4))
            u_t = jnp.pad(u_t, ((0, 0), (0, 0), (0, nb - n_seq), (0, 0), (0, 0))).reshape(g5, n_c * nb, q5 * i5)
            h0 = jnp.zeros((g5, nb, 2 * p5), F32)
            y_t, hf = _s5_scan(u_t, ops, h0, n_c, nb)
            y5 = jnp.transpose(y_t.reshape(g5, n_c, nb, q5, i5)[:, :, :n_seq], (2, 1, 3, 0, 4)).reshape(m, g5 * i5)
        else:
            nb, kq = n_seq, LANES
            ops = _s5_operators(*s5_params(lyr), 1, kq)
            u_t = jnp.pad(jnp.transpose(u.reshape(n_seq, g5, i5), (1, 0, 2)), ((0, 0), (0, 0), (0, kq - i5)))
            h0 = jnp.concatenate([jnp.transpose(states["s5_re"][lyr], (1, 0, 2)),
                                  jnp.transpose(states["s5_im"][lyr], (1, 0, 2))], axis=-1)
            y_t, hf = _s5_scan(u_t, ops, h0, 1, nb)
            y5 = jnp.transpose(y_t[:, :, :i5], (1, 0, 2)).reshape(m, g5 * i5)
        outs["s5_re"].append(jnp.transpose(hf[:, :n_seq, :p5], (1, 0, 2)))
        outs["s5_im"].append(jnp.transpose(hf[:, :n_seq, p5:], (1, 0, 2)))
        o_a = _s5_glu(y5, u, s5_d, w_glu, lyr)

        if prompt:
            n_blk = seq // wn
            cur = lambda b, n: b * n_blk + n
            prev = lambda b, n: b * n_blk + jnp.maximum(n - 1, 0)
            k_col, v_col = qw // kw, qw // kw + 1
            o_b = _swa(qkv, 0, (qkv, prev, k_col), (qkv, cur, k_col), (qkv, prev, v_col), (qkv, cur, v_col),
                       bias_tab, attn_sink[lyr], n_seq, n_blk, n_kv, hd, first_has_prev=False)
            kv3 = qkv.reshape(n_seq, seq, qw + 2 * kw)
            outs["k"].append(kv3[:, seq - wn:, qw:qw + kw].reshape(n_seq, wn, n_kv, hd))
            outs["v"].append(kv3[:, seq - wn:, qw + kw:].reshape(n_seq, wn, n_kv, hd))

            n_chunks = seq // SSD_CHUNK
            conv0 = jnp.zeros((n_seq, SUBLANES, ch), F32)
            h0 = jnp.zeros((n_seq, n_heads_ssd, hp, ns), F32)
            o_c, ssm_f = _ssd(xbc, z, dt_raw, conv0, h0, ssd_params, lyr, n_seq, n_chunks, SSD_CHUNK)
            outs["conv"].append(xbc.reshape(n_seq, seq, ch)[:, seq - (taps - 1):])
        else:
            one = lambda b, n: b
            q_pad = _pad_rows_per_seq(qkv[:, :qw], wn)
            k_new, v_new = qkv[:, qw:qw + kw], qkv[:, qw + kw:]
            k_cache = states["k"][lyr].reshape(n_seq * wn, kw)
            v_cache = states["v"][lyr].reshape(n_seq * wn, kw)
            o_b = _swa(q_pad, 0, (k_cache, one, 0), (_pad_rows_per_seq(k_new, wn), one, 0),
                       (v_cache, one, 0), (_pad_rows_per_seq(v_new, wn), one, 0),
                       bias_tab, attn_sink[lyr], n_seq, 1, n_kv, hd, first_has_prev=True)
            o_b = o_b.reshape(n_seq, wn, qw)[:, 0]
            outs["k"].append(jnp.concatenate([states["k"][lyr][:, 1:], k_new.reshape(n_seq, 1, n_kv, hd)], axis=1))
            outs["v"].append(jnp.concatenate([states["v"][lyr][:, 1:], v_new.reshape(n_seq, 1, n_kv, hd)], axis=1))

            cq = SSD_CHUNK
            conv_hist = states["conv"][lyr]
            conv0 = jnp.pad(conv_hist, ((0, 0), (SUBLANES - (taps - 1), 0), (0, 0)))
            o_c, ssm_f = _ssd(_pad_rows_per_seq(xbc, cq), _pad_rows_per_seq(z, cq), _pad_rows_per_seq(dt_raw, cq),
                              conv0, states["ssm"][lyr], ssd_params, lyr, n_seq, 1, 1)
            o_c = o_c.reshape(n_seq, cq, -1)[:, 0]
            outs["conv"].append(jnp.concatenate([conv_hist[:, 1:], xbc[:, None, :]], axis=1))
        outs["ssm"].append(ssm_f)

        merged = _merge(o_a, o_b, o_c, w_branch, lyr, gates)
        y = _mm(merged, w_out, lyr)
        x, h = _deepnorm(x, y, mods, ls + 1, ln_g, ln_b, ls + 2, alpha, 1.0)

        y = _mm(_ffn_up(h, w1, w3, 2 * lyr + 1), w2, 2 * lyr + 1)
        next_ls = ls + 3 if lyr + 1 < depth else None
        x, h = _deepnorm(x, y, mods, ls + 2, ln_g, ln_b, next_ls, alpha, 0.5)

    stack = lambda name: jnp.stack(outs[name])
    return x, stack("s5_re"), stack("s5_im"), stack("ssm"), stack("conv"), stack("k"), stack("v")


def kernel(x_prompt, x_sample, state_s5_re, state_s5_im, state_ssm, state_conv, cache_k_win, cache_v_win,
           c_prompt, c_sample, w_ada, b_ada, ln_g, ln_b, ffn_w1, ffn_w3, ffn_w2, w_in, w_branch, w_out,
           s5_lam_re, s5_lam_im, s5_log_dt, s5_b_re, s5_b_im, s5_c_re, s5_c_im, s5_d, s5_w_glu,
           attn_sink, rel_bias, conv_w, conv_b, dt_bias, a_log, ssd_d, ssd_norm_w):
    batch, seq, d = x_prompt.shape
    dec_b, dec_seq, _ = x_sample.shape
    assert dec_seq == 1, "the sample group decodes one token per sequence"
    depth, n_sub = w_ada.shape[:2]
    d_ff = ffn_w1.shape[-1]
    g5, p5, i5 = s5_b_re.shape[1:]
    s5_w = g5 * i5
    n_heads = attn_sink.shape[1]
    _, _, wn, n_kv, hd = cache_k_win.shape
    qw, kw = n_heads * hd, n_kv * hd
    _, _, ssd_heads, hp, ns = state_ssm.shape
    ssd_w = ssd_heads * hp
    taps, ch = conv_w.shape[1:]
    n_groups = (ch - ssd_w) // (2 * ns)
    assert seq % wn == 0 and seq % SSD_CHUNK == 0 and seq % S5_CHUNK == 0
    assert ssd_heads <= LANES and taps - 1 <= SUBLANES

    bf = lambda w: w.astype(BF16)
    w1 = bf(ffn_w1).reshape(depth * 2, d, d_ff)
    w3 = bf(ffn_w3).reshape(depth * 2, d, d_ff)
    w2 = bf(ffn_w2).reshape(depth * 2, d_ff, d)
    o_q = s5_w
    o_z = o_q + qw + 2 * kw
    o_xbc = o_z + ssd_w
    o_dt = o_xbc + ch
    o_g = o_dt + ssd_heads
    w_u, w_qkv, w_z, w_xbc = bf(w_in[:, :, :o_q]), bf(w_in[:, :, o_q:o_z]), bf(w_in[:, :, o_z:o_xbc]), bf(w_in[:, :, o_xbc:o_dt])
    w_dt = jnp.pad(bf(w_in[:, :, o_dt:o_g]), ((0, 0), (0, 0), (0, LANES - ssd_heads)))
    w_gates = bf(w_in[:, :, o_g:])
    w_branch_b, w_out_b, w_glu = bf(w_branch), bf(w_out), bf(s5_w_glu)

    lane_pad = lambda v: jnp.pad(v, ((0, 0), (0, LANES - v.shape[1])))[:, None, :]
    ssd_params = (conv_w, conv_b[:, None, :], lane_pad(dt_bias), dt_bias[:, :, None],
                  lane_pad(a_log), a_log[:, :, None], jnp.repeat(ssd_d, hp, axis=1)[:, None, :],
                  ssd_norm_w[:, None, :], n_groups)
    s5_params = lambda lyr: (s5_lam_re[lyr], s5_lam_im[lyr], s5_log_dt[lyr], s5_b_re[lyr], s5_b_im[lyr],
                             s5_c_re[lyr], s5_c_im[lyr])
    bias_tab = _rel_bias_table(rel_bias, wn, n_kv)
    ln_g3 = ln_g.reshape(depth * n_sub, 1, d)
    ln_b3 = ln_b.reshape(depth * n_sub, 1, d)
    wts = (w1, w3, w2, w_u, w_qkv, w_z, w_xbc, w_dt, w_gates, w_branch_b, w_out_b, w_glu,
           ln_g3, ln_b3, s5_params, s5_d[:, None, :], attn_sink, bias_tab, ssd_params)

    rows = -(-(dec_b + batch) // SUBLANES) * SUBLANES
    c_all = jnp.pad(jnp.concatenate([c_sample, c_prompt], axis=0), ((0, rows - dec_b - batch), (0, 0)))
    mods = _adaln(c_all, w_ada.reshape(depth * n_sub, d, 3 * d), b_ada.reshape(depth * n_sub, 1, 3 * d))

    dims = dict(depth=depth, d=d, n_kv=n_kv, hd=hd, window=wn, s5_groups=g5, s5_group=i5, s5_state=p5,
                ssd_heads=ssd_heads, ssd_hp=hp, ssd_state=ns, taps=taps, conv_ch=ch)
    p_out = _run_trunk(x_prompt.reshape(batch * seq, d), _Mods(mods, True, dec_b, seq, d), wts,
                       dict(dims, n_seq=batch, seq=seq), None)
    states = dict(s5_re=state_s5_re, s5_im=state_s5_im, ssm=state_ssm, conv=state_conv, k=cache_k_win, v=cache_v_win)
    s_out = _run_trunk(x_sample.reshape(dec_b, d), _Mods(mods, False, dec_b, 1, d), wts,
                       dict(dims, n_seq=dec_b, seq=1), states)
    y_prompt = p_out[0].reshape(batch, seq, d)
    y_sample = s_out[0].reshape(dec_b, 1, d)
    return (y_prompt, y_sample) + tuple(p_out[1:]) + tuple(s_out[1:])
```

```python
import functools
import math

import jax
import jax.numpy as jnp
from jax import lax
from jax.experimental import pallas as pl
from jax.experimental.pallas import tpu as pltpu

F32 = jnp.float32
BF16 = jnp.bfloat16

LN_EPS = 1e-5
RMS_EPS = 1e-5
SSD_CHUNK = 128
S5_CHUNK = 16
LANES = 128
SUBLANES = 8
SAMPLE_Q_ROWS = 16
VMEM_BYTES_V7X = 64 * 1024 * 1024
VMEM_LIMIT = VMEM_BYTES_V7X - 8 * 1024 * 1024
NEG_BIG = -1e30

_NT = (((1,), (1,)), ((), ()))
_TN = (((0,), (0,)), ((), ()))


def _pick_tile(n, cap, align):
    if n <= cap:
        return n
    t = (cap // align) * align
    while t >= align:
        if n % t == 0:
            return t
        t -= align
    raise ValueError(f"no tile for {n} under {cap} aligned to {align}")


def _cparams(*sem):
    return pltpu.CompilerParams(dimension_semantics=sem, vmem_limit_bytes=VMEM_LIMIT)


def _sigmoid(x):
    return 1.0 / (1.0 + jnp.exp(-x))


def _silu(x):
    return x * _sigmoid(x)


def _softplus(x):
    return jnp.maximum(x, 0.0) + jnp.log1p(jnp.exp(-jnp.abs(x)))


def _dot(a, b, precision=None):
    return jnp.dot(a, b, preferred_element_type=F32, precision=precision)


def _adaln_kernel(c_ref, w_ref, b_ref, o_ref):
    a = _silu(c_ref[...]).astype(BF16)
    o_ref[...] = _dot(a, w_ref[...].astype(BF16)) + b_ref[...]


def _adaln(c_all, w_ada, b_ada):
    n_ls, d, n3 = w_ada.shape
    rows = c_all.shape[0]
    tn = _pick_tile(n3, 512, LANES)
    return pl.pallas_call(
        _adaln_kernel,
        grid=(n_ls, n3 // tn),
        in_specs=[pl.BlockSpec((rows, d), lambda s, j: (0, 0)),
                  pl.BlockSpec((None, d, tn), lambda s, j: (s, 0, j)),
                  pl.BlockSpec((None, 1, tn), lambda s, j: (s, 0, j))],
        out_specs=pl.BlockSpec((None, rows, tn), lambda s, j: (s, 0, j)),
        out_shape=jax.ShapeDtypeStruct((n_ls, rows, n3), F32),
        compiler_params=_cparams("arbitrary", "arbitrary"),
    )(c_all, w_ada, b_ada)


class _Mods:
    def __init__(self, mods, prompt, dec_b, seq, d):
        self.prompt, self.dec_b, self.seq, self.d = prompt, dec_b, seq, d
        n_ls, rows, n3 = mods.shape
        self.arr = mods.reshape(n_ls, rows, 1, n3) if prompt else mods

    def spec(self, ls, part, tm):
        d = self.d
        if self.prompt:
            dec_b, seq = self.dec_b, self.seq
            return pl.BlockSpec((None, None, 1, d), lambda i, *_: (ls, dec_b + (i * tm) // seq, 0, part))
        return pl.BlockSpec((None, self.dec_b, d), lambda i, *_: (ls, 0, part))


def _modulate_kernel(x_ref, sh_ref, sc_ref, h_ref):
    h_ref[...] = (x_ref[...] * (1.0 + sc_ref[...]) + sh_ref[...]).astype(BF16)


def _modulate(x, mods, ls):
    m, d = x.shape
    tm = _pick_tile(mods.seq, 256, 16) if mods.prompt else m
    return pl.pallas_call(
        _modulate_kernel,
        grid=(m // tm,),
        in_specs=[pl.BlockSpec((tm, d), lambda i: (i, 0)), mods.spec(ls, 0, tm), mods.spec(ls, 1, tm)],
        out_specs=pl.BlockSpec((tm, d), lambda i: (i, 0)),
        out_shape=jax.ShapeDtypeStruct((m, d), BF16),
        compiler_params=_cparams("arbitrary"),
    )(x, mods.arr, mods.arr)


def _deepnorm_kernel(x_ref, y_ref, gate_ref, g_ref, b_ref, *rest, alpha, coef, with_h):
    t = alpha * x_ref[...] + (1.0 + gate_ref[...]) * (coef * y_ref[...])
    mu = jnp.mean(t, axis=-1, keepdims=True)
    dlt = t - mu
    var = jnp.mean(dlt * dlt, axis=-1, keepdims=True)
    xn = dlt * lax.rsqrt(var + LN_EPS) * g_ref[...] + b_ref[...]
    if with_h:
        sh_ref, sc_ref, xo_ref, ho_ref = rest
        ho_ref[...] = (xn * (1.0 + sc_ref[...]) + sh_ref[...]).astype(BF16)
    else:
        (xo_ref,) = rest
    xo_ref[...] = xn


def _deepnorm(x, y, mods, ls, ln_g, ln_b, next_ls, alpha, coef):
    m, d = x.shape
    tm = _pick_tile(mods.seq, 256, 16) if mods.prompt else m
    with_h = next_ls is not None
    row = pl.BlockSpec((tm, d), lambda i: (i, 0))
    vec = pl.BlockSpec((None, 1, d), lambda i: (ls, 0, 0))
    in_specs = [row, row, mods.spec(ls, 2, tm), vec, vec]
    args = [x, y, mods.arr, ln_g, ln_b]
    out_specs, out_shape = [row], [jax.ShapeDtypeStruct((m, d), F32)]
    if with_h:
        in_specs += [mods.spec(next_ls, 0, tm), mods.spec(next_ls, 1, tm)]
        args += [mods.arr, mods.arr]
        out_specs.append(row)
        out_shape.append(jax.ShapeDtypeStruct((m, d), BF16))
    res = pl.pallas_call(
        functools.partial(_deepnorm_kernel, alpha=alpha, coef=coef, with_h=with_h),
        grid=(m // tm,), in_specs=in_specs, out_specs=out_specs, out_shape=out_shape,
        compiler_params=_cparams("arbitrary"),
    )(*args)
    return (res[0], res[1]) if with_h else (res[0], None)


def _col_slab(j, tn):
    return pl.ds(pl.multiple_of(j * tn, tn), tn)


def _mm_kernel(x_ref, xs_ref, w_ref, o_ref, os_ref, *, act, tn):
    def apply(x):
        r = _dot(x, w_ref[...].astype(BF16))
        return _sigmoid(r) if act == "sigmoid" else r

    o_ref[...] = apply(x_ref[...]).astype(o_ref.dtype)

    @pl.when(pl.program_id(0) == 0)
    def _():
        os_ref[:, _col_slab(pl.program_id(1), tn)] = apply(xs_ref[...]).astype(os_ref.dtype)


def _row_tile(m, k):
    cap = 2048 if k <= 4096 else 512
    return _pick_tile(m, cap, 16)


def _row_spec(tm, k, single):
    mode = dict(pipeline_mode=pl.Buffered(1)) if single else {}
    return pl.BlockSpec((tm, k), lambda i, j: (i, 0), **mode)


def _full_spec(shape):
    return pl.BlockSpec(shape, lambda i, j: (0,) * len(shape))


def _mm(x, xs, w, widx, col_off=0, n=None, tn_cap=512, act=None, out_dtype=F32):
    m, k = x.shape
    n = w.shape[-1] if n is None else n
    tm, tn = _row_tile(m, k), _pick_tile(n, tn_cap, LANES)
    assert col_off % tn == 0, "column window must start on a tile boundary"
    col0 = col_off // tn
    single = tm * k * 2 > 12 * 1024 * 1024
    ms = xs.shape[0]
    return pl.pallas_call(
        functools.partial(_mm_kernel, act=act, tn=tn),
        grid=(m // tm, n // tn),
        in_specs=[_row_spec(tm, k, single), _full_spec((ms, k)),
                  pl.BlockSpec((None, k, tn), lambda i, j: (widx, 0, col0 + j))],
        out_specs=[pl.BlockSpec((tm, tn), lambda i, j: (i, j)), _full_spec((ms, n))],
        out_shape=[jax.ShapeDtypeStruct((m, n), out_dtype), jax.ShapeDtypeStruct((ms, n), out_dtype)],
        compiler_params=_cparams("arbitrary", "arbitrary"),
    )(x, xs, w)


def _ffn_up_kernel(h_ref, hs_ref, w1_ref, w3_ref, o_ref, os_ref, *, tn):
    def apply(h):
        a = _dot(h, w1_ref[...].astype(BF16))
        return (_silu(a) * _dot(h, w3_ref[...].astype(BF16))).astype(BF16)

    o_ref[...] = apply(h_ref[...])

    @pl.when(pl.program_id(0) == 0)
    def _():
        os_ref[:, _col_slab(pl.program_id(1), tn)] = apply(hs_ref[...])


def _ffn_up(h, hs, w1, w3, widx):
    m, k = h.shape
    n = w1.shape[-1]
    tm, tn = _row_tile(m, k), _pick_tile(n, 256, LANES)
    ms = hs.shape[0]
    wspec = pl.BlockSpec((None, k, tn), lambda i, j: (widx, 0, j))
    return pl.pallas_call(
        functools.partial(_ffn_up_kernel, tn=tn),
        grid=(m // tm, n // tn),
        in_specs=[_row_spec(tm, k, tm * k * 2 > 12 * 1024 * 1024), _full_spec((ms, k)), wspec, wspec],
        out_specs=[pl.BlockSpec((tm, tn), lambda i, j: (i, j)), _full_spec((ms, n))],
        out_shape=[jax.ShapeDtypeStruct((m, n), BF16), jax.ShapeDtypeStruct((ms, n), BF16)],
        compiler_params=_cparams("arbitrary", "arbitrary"),
    )(h, hs, w1, w3)


def _merge_kernel(oa_ref, ob_ref, oc_ref, sa_ref, sb_ref, sc_ref, wa_ref, wb_ref, wc_ref,
                  ga_ref, gb_ref, gc_ref, gs_ref, o_ref, os_ref, *, tn, d):
    def apply(ga, gb, gc, a, b, c):
        return (ga * _dot(a, wa_ref[...]) + gb * _dot(b, wb_ref[...]) + gc * _dot(c, wc_ref[...])).astype(BF16)

    o_ref[...] = apply(ga_ref[...], gb_ref[...], gc_ref[...], oa_ref[...], ob_ref[...], oc_ref[...])

    @pl.when(pl.program_id(0) == 0)
    def _():
        j = pl.program_id(1)
        gate = lambda part: gs_ref[:, pl.ds(pl.multiple_of(part * d + j * tn, tn), tn)]
        os_ref[:, _col_slab(j, tn)] = apply(gate(0), gate(1), gate(2), sa_ref[...], sb_ref[...], sc_ref[...])


def _merge(branches_p, branches_s, w_branch, lyr, gates_p, gates_s):
    o_a, o_b, o_c = branches_p
    m, ka = o_a.shape
    kb, kc = o_b.shape[1], o_c.shape[1]
    d = w_branch.shape[-1]
    ms = gates_s.shape[0]
    assert ka == kb and kc % (ka + kb) == 0, "branch widths must tile the stacked branch weight"
    tm, tn = _pick_tile(m, 1024, 16), _pick_tile(d, 512, LANES)
    nj = d // tn
    rows = lambda width: pl.BlockSpec((tm, width), lambda i, j: (i, 0))
    return pl.pallas_call(
        functools.partial(_merge_kernel, tn=tn, d=d),
        grid=(m // tm, nj),
        in_specs=[rows(ka), rows(kb), rows(kc),
                  _full_spec((ms, ka)), _full_spec((ms, kb)), _full_spec((ms, kc)),
                  pl.BlockSpec((None, ka, tn), lambda i, j: (lyr, 0, j)),
                  pl.BlockSpec((None, kb, tn), lambda i, j: (lyr, 1, j)),
                  pl.BlockSpec((None, kc, tn), lambda i, j: (lyr, (ka + kb) // kc, j)),
                  pl.BlockSpec((tm, tn), lambda i, j: (i, j)),
                  pl.BlockSpec((tm, tn), lambda i, j: (i, nj + j)),
                  pl.BlockSpec((tm, tn), lambda i, j: (i, 2 * nj + j)),
                  _full_spec((ms, 3 * d))],
        out_specs=[pl.BlockSpec((tm, tn), lambda i, j: (i, j)), _full_spec((ms, d))],
        out_shape=[jax.ShapeDtypeStruct((m, d), BF16), jax.ShapeDtypeStruct((ms, d), BF16)],
        compiler_params=_cparams("arbitrary", "arbitrary"),
    )(o_a, o_b, o_c, *branches_s, w_branch, w_branch, w_branch, gates_p, gates_p, gates_p, gates_s)


def _cmul(ar, ai, br, bi):
    return ar * br - ai * bi, ar * bi + ai * br


def _gelu_tanh(x):
    return jax.nn.gelu(x, approximate=True)


def _s5_tile_kernel(u_ref, kd_ref, bd_ref, cd_ref, apr_ref, api_ref, d_ref, a_ref, hr_ref, hi_ref,
                    w_scr, shi_scr, slo_scr, e_scr, hp_scr, y_scr, *, q, n_chunks, nb):
    lt = LANES
    sw = bd_ref.shape[-1] // 2
    zero = jnp.zeros((lt, lt), BF16)
    for s in range(q):
        for l in range(q):
            w_scr[s * lt:(s + 1) * lt, l * lt:(l + 1) * lt] = kd_ref[l - s] if l >= s else zero
    b_re, b_im = bd_ref[:, :sw], bd_ref[:, sw:]
    for s in range(q):
        s_re, s_im = _cmul(b_re, b_im, apr_ref[q - 1 - s:q - s, :], api_ref[q - 1 - s:q - s, :])
        s_full = jnp.concatenate([s_re, s_im], axis=1)
        s_hi = s_full.astype(BF16)
        shi_scr[s * lt:(s + 1) * lt, :] = s_hi
        slo_scr[s * lt:(s + 1) * lt, :] = (s_full - s_hi.astype(F32)).astype(BF16)
    u = u_ref[...]
    u_hi = u.astype(BF16)
    u_lo = (u - u_hi.astype(F32)).astype(BF16)
    e_scr[...] = _dot(u_hi, shi_scr[...]) + _dot(u_lo, shi_scr[...]) + _dot(u_hi, slo_scr[...])
    aq_re, aq_im = apr_ref[q:q + 1, :], api_ref[q:q + 1, :]
    h_re = jnp.zeros((nb, sw), F32)
    h_im = jnp.zeros((nb, sw), F32)
    for c in range(n_chunks):
        rows = slice(c * nb, (c + 1) * nb)
        hp_scr[rows, :sw] = h_re
        hp_scr[rows, sw:] = h_im
        t_re, t_im = _cmul(h_re, h_im, aq_re, aq_im)
        h_re, h_im = t_re + e_scr[rows, :sw], t_im + e_scr[rows, sw:]
    hr_ref[...] = h_re
    hi_ref[...] = h_im
    y_scr[...] = _dot(u_hi, w_scr[...])
    p_re, p_im = hp_scr[:, :sw], hp_scr[:, sw:]
    cd = cd_ref[...]
    for l in range(q):
        t_re, t_im = _cmul(p_re, p_im, apr_ref[l + 1:l + 2, :], api_ref[l + 1:l + 2, :])
        adv = jnp.concatenate([t_re, t_im], axis=1).astype(BF16)
        y_scr[:, l * lt:(l + 1) * lt] += _dot(adv, cd)
    a_ref[...] = _gelu_tanh(y_scr[...] + d_ref[...] * u).astype(BF16)


def _s5_prompt(u, ops, n_seq, seq):
    kd, bd, cd, ap_re, ap_im, d_t, gpt, p = ops
    q = S5_CHUNK
    n_t, n_c, width = kd.shape[1], seq // q, u.shape[1]
    rows, cols, sw = n_c * n_seq, q * LANES, bd.shape[-1] // 2
    u_t = jnp.transpose(u.reshape(n_seq, n_c, q, n_t, LANES), (3, 1, 0, 2, 4)).reshape(n_t, rows, cols)
    blk = lambda *shape: pl.BlockSpec((None,) + shape, lambda i: (i,) + (0,) * len(shape))
    a_t, h_re, h_im = pl.pallas_call(
        functools.partial(_s5_tile_kernel, q=q, n_chunks=n_c, nb=n_seq),
        grid=(n_t,),
        in_specs=[blk(rows, cols), pl.BlockSpec((q, None, LANES, LANES), lambda i: (0, i, 0, 0)),
                  blk(LANES, 2 * sw), blk(2 * sw, LANES), blk(q + 1, sw), blk(q + 1, sw), blk(1, cols)],
        out_specs=[blk(rows, cols), blk(n_seq, sw), blk(n_seq, sw)],
        out_shape=[jax.ShapeDtypeStruct((n_t, rows, cols), BF16),
                   jax.ShapeDtypeStruct((n_t, n_seq, sw), F32), jax.ShapeDtypeStruct((n_t, n_seq, sw), F32)],
        scratch_shapes=[pltpu.VMEM((cols, cols), BF16), pltpu.VMEM((cols, 2 * sw), BF16),
                        pltpu.VMEM((cols, 2 * sw), BF16), pltpu.VMEM((rows, 2 * sw), F32),
                        pltpu.VMEM((rows, 2 * sw), F32), pltpu.VMEM((rows, cols), F32)],
        compiler_params=_cparams("arbitrary"),
    )(u_t, kd, bd, cd, ap_re, ap_im, d_t)
    a = jnp.transpose(a_t.reshape(n_t, n_c, n_seq, q, LANES), (2, 1, 3, 0, 4)).reshape(n_seq * seq, width)
    to_state = lambda h: jnp.transpose(h.reshape(n_t, n_seq, gpt, p), (1, 0, 2, 3)).reshape(n_seq, n_t * gpt, p)
    return a, to_state(h_re), to_state(h_im)


def _s5_step_kernel(u_ref, st_ref, tt_ref, rt_ref, a1_ref, a2_ref, d_ref, h0_ref, a_ref, hf_ref):
    hi = lax.Precision.HIGHEST
    u, h0 = u_ref[...], h0_ref[...]
    half = h0.shape[-1] // 2
    hf_ref[...] = h0 * a1_ref[...] + pltpu.roll(h0, half, axis=1) * a2_ref[...] + _dot(u, st_ref[...], hi)
    y = _dot(u, tt_ref[...], hi) + _dot(h0, rt_ref[...], hi)
    a_ref[...] = _gelu_tanh(y + d_ref[...] * u).astype(BF16)


def _s5_sample(u, ops, s5_re, s5_im):
    st, tt, rt, a1, a2, d_g = ops
    g, kq, p2 = st.shape
    nb, i_ch = u.shape[0], u.shape[1] // g
    u_t = jnp.pad(jnp.transpose(u.reshape(nb, g, i_ch), (1, 0, 2)), ((0, 0), (0, 0), (0, kq - i_ch)))
    h0 = jnp.concatenate([jnp.transpose(s5_re, (1, 0, 2)), jnp.transpose(s5_im, (1, 0, 2))], axis=-1)
    blk = lambda *shape: pl.BlockSpec((None,) + shape, lambda i: (i,) + (0,) * len(shape))
    a_t, hf = pl.pallas_call(
        _s5_step_kernel,
        grid=(g,),
        in_specs=[blk(nb, kq), blk(kq, p2), blk(kq, kq), blk(p2, kq), blk(1, p2), blk(1, p2), blk(1, kq), blk(nb, p2)],
        out_specs=[blk(nb, kq), blk(nb, p2)],
        out_shape=[jax.ShapeDtypeStruct((g, nb, kq), BF16), jax.ShapeDtypeStruct((g, nb, p2), F32)],
        compiler_params=_cparams("arbitrary"),
    )(u_t, st, tt, rt, a1, a2, d_g, h0)
    a = jnp.transpose(a_t[:, :, :i_ch], (1, 0, 2)).reshape(nb, g * i_ch)
    half = p2 // 2
    return a, jnp.transpose(hf[:, :, :half], (1, 0, 2)), jnp.transpose(hf[:, :, half:], (1, 0, 2))


def _s5_basis(lam_re, lam_im, log_dt, b_re, b_im, c_re, c_im, q):
    hi = lax.Precision.HIGHEST
    dt = jnp.exp(log_dt)[:, None]
    t = jnp.arange(q + 1, dtype=F32)[:, None, None]
    mag = jnp.exp(lam_re[None] * dt[None] * t)
    ang = lam_im[None] * dt[None] * t
    ap_re, ap_im = mag * jnp.cos(ang), mag * jnp.sin(ang)
    nr, ni = ap_re[1] - 1.0, ap_im[1]
    den = lam_re * lam_re + lam_im * lam_im
    f_re, f_im = (nr * lam_re + ni * lam_im) / den, (ni * lam_re - nr * lam_im) / den
    bb_re, bb_im = _cmul(f_re[..., None], f_im[..., None], b_re, b_im)
    ab_re, ab_im = _cmul(ap_re[:q, :, :, None], ap_im[:q, :, :, None], bb_re[None], bb_im[None])
    k_t = (jnp.einsum("gip,tgpj->tgij", c_re, ab_re, precision=hi)
           - jnp.einsum("gip,tgpj->tgij", c_im, ab_im, precision=hi))
    return ap_re, ap_im, bb_re, bb_im, k_t


def _s5_prompt_operators(params, d_skip):
    lam_re, lam_im, log_dt, b_re, b_im, c_re, c_im = params
    q = S5_CHUNK
    g, p, i_ch = b_re.shape
    gpt = LANES // i_ch
    n_t = g // gpt
    ap_re, ap_im, bb_re, bb_im, k_t = _s5_basis(*params, q)
    eye = jnp.eye(gpt, dtype=F32)
    kd = jnp.einsum("tqgij,gh->tqgjhi", k_t.reshape(q, n_t, gpt, i_ch, i_ch), eye).reshape(q, n_t, LANES, LANES)
    to_bd = lambda v: jnp.einsum("qgpj,gh->qgjhp", v.reshape(n_t, gpt, p, i_ch), eye).reshape(n_t, LANES, gpt * p)
    bd = jnp.concatenate([to_bd(bb_re), to_bd(bb_im)], axis=-1)
    to_cd = lambda v: jnp.einsum("qgip,gh->qgphi", v.reshape(n_t, gpt, i_ch, p), eye).reshape(n_t, gpt * p, LANES)
    cd = jnp.concatenate([to_cd(c_re), -to_cd(c_im)], axis=1)
    to_t = lambda v: jnp.transpose(v.reshape(q + 1, n_t, gpt * p), (1, 0, 2))
    d_t = jnp.tile(d_skip.reshape(n_t, 1, LANES), (1, 1, q))
    return kd.astype(BF16), bd, cd.astype(BF16), to_t(ap_re), to_t(ap_im), d_t, gpt, p


def _s5_sample_operators(params, d_skip):
    ap_re, ap_im, bb_re, bb_im, k_t = _s5_basis(*params, 1)
    c_re, c_im = params[5], params[6]
    g, p, i_ch = bb_re.shape
    pad = LANES - i_ch
    st = jnp.pad(jnp.concatenate([jnp.transpose(bb_re, (0, 2, 1)), jnp.transpose(bb_im, (0, 2, 1))], axis=-1),
                 ((0, 0), (0, pad), (0, 0)))
    tt = jnp.pad(jnp.transpose(k_t[0], (0, 2, 1)), ((0, 0), (0, pad), (0, pad)))
    rc_re, rc_im = _cmul(c_re, c_im, ap_re[1][:, None, :], ap_im[1][:, None, :])
    rt = jnp.pad(jnp.concatenate([jnp.transpose(rc_re, (0, 2, 1)), -jnp.transpose(rc_im, (0, 2, 1))], axis=1),
                 ((0, 0), (0, 0), (0, pad)))
    a1 = jnp.concatenate([ap_re[1], ap_re[1]], axis=-1)[:, None, :]
    a2 = jnp.concatenate([-ap_im[1], ap_im[1]], axis=-1)[:, None, :]
    d_g = jnp.pad(d_skip.reshape(g, 1, i_ch), ((0, 0), (0, 0), (0, pad)))
    return st, tt, rt, a1, a2, d_g


def _s5_glu_kernel(a_ref, w_ref, o_ref):
    r = _dot(a_ref[...], w_ref[...])
    wd = o_ref.shape[-1]
    o_ref[...] = (r[:, :wd] * _sigmoid(r[:, wd:])).astype(BF16)


def _s5_glu(a, w_glu, lyr):
    m, wd = a.shape
    tm = _pick_tile(m, 1024, 16)
    row = pl.BlockSpec((tm, wd), lambda i: (i, 0))
    return pl.pallas_call(
        _s5_glu_kernel,
        grid=(m // tm,),
        in_specs=[row, pl.BlockSpec((None, wd, 2 * wd), lambda i: (lyr, 0, 0))],
        out_specs=row,
        out_shape=jax.ShapeDtypeStruct((m, wd), BF16),
        compiler_params=_cparams("arbitrary"),
    )(a, w_glu)


def _swa_kernel(sink_ref, q_ref, kp_ref, kc_ref, vp_ref, vc_ref, bias_ref, o_ref,
                *, n_kv, rep, hd, first_has_prev, scale):
    n = pl.program_id(1)
    wq, wn = q_ref.shape[0], kc_ref.shape[0]
    q = q_ref[...]
    k = jnp.concatenate([kp_ref[...], kc_ref[...]], axis=0).astype(BF16)
    v = jnp.concatenate([vp_ref[...], vc_ref[...]], axis=0).astype(BF16)
    l_i = lax.broadcasted_iota(jnp.int32, (rep, wq, 2 * wn), 1).reshape(rep * wq, 2 * wn)
    k_j = lax.broadcasted_iota(jnp.int32, (rep * wq, 2 * wn), 1)
    dist = l_i + wn - k_j
    valid = (dist >= 0) & (dist < wn)
    if not first_has_prev:
        valid = valid & ((n > 0) | (k_j >= wn))
    for h in range(n_kv):
        heads = [h * rep + r for r in range(rep)]
        qs = jnp.concatenate([q[:, a * hd:(a + 1) * hd] for a in heads], axis=0).astype(BF16)
        kh, vh = k[:, h * hd:(h + 1) * hd], v[:, h * hd:(h + 1) * hd]
        s = lax.dot_general(qs, kh, _NT, preferred_element_type=F32) * scale + bias_ref[h]
        s = jnp.where(valid, s, NEG_BIG)
        sink = jnp.concatenate([jnp.full((wq, 1), sink_ref[a], F32) for a in heads], axis=0)
        mx = jnp.maximum(jnp.max(s, axis=1, keepdims=True), sink)
        p = jnp.exp(s - mx)
        den = jnp.sum(p, axis=1, keepdims=True) + jnp.exp(sink - mx)
        o = _dot(p.astype(BF16), vh) / den
        for r, a in enumerate(heads):
            o_ref[:, a * hd:(a + 1) * hd] = o[r * wq:(r + 1) * wq].astype(BF16)


def _swa(q_arr, kp, kc, vp, vc, bias, sink, n_seq, n_blk, wn, n_kv, hd, first_has_prev):
    n_heads = sink.shape[0]
    rep = n_heads // n_kv
    wq = bias.shape[1] // rep
    qw, kw = n_heads * hd, n_kv * hd

    def kv_spec(src):
        _, rmap, col = src
        return pl.BlockSpec((wn, kw), lambda b, n: (rmap(b, n), col))

    return pl.pallas_call(
        functools.partial(_swa_kernel, n_kv=n_kv, rep=rep, hd=hd, first_has_prev=first_has_prev,
                          scale=hd ** -0.5),
        grid=(n_seq, n_blk),
        in_specs=[pl.BlockSpec(memory_space=pltpu.SMEM),
                  pl.BlockSpec((wq, qw), lambda b, n: (b * n_blk + n, 0)),
                  kv_spec(kp), kv_spec(kc), kv_spec(vp), kv_spec(vc),
                  pl.BlockSpec(bias.shape, lambda b, n: (0, 0, 0))],
        out_specs=pl.BlockSpec((wq, qw), lambda b, n: (b * n_blk + n, 0)),
        out_shape=jax.ShapeDtypeStruct((n_seq * n_blk * wq, qw), BF16),
        compiler_params=_cparams("arbitrary", "arbitrary"),
    )(sink, q_arr, kp[0], kc[0], vp[0], vc[0], bias)


def _rel_bias_table(rel_bias, wn, n_kv):
    n_buckets, n_heads = rel_bias.shape
    max_exact = n_buckets // 2
    dist = (jnp.arange(wn)[:, None] + wn) - jnp.arange(2 * wn)[None, :]
    dist = jnp.clip(dist, 0, wn - 1)
    d = jnp.maximum(dist, 1).astype(F32)
    large = max_exact + (jnp.log(d / max_exact) / math.log(wn / max_exact)
                         * (n_buckets - max_exact)).astype(jnp.int32)
    bucket = jnp.where(dist < max_exact, dist, jnp.minimum(large, n_buckets - 1))
    bias = jnp.transpose(rel_bias[bucket], (2, 0, 1)).astype(F32)
    return bias.reshape(n_kv, (n_heads // n_kv) * wn, 2 * wn)


def _ssd_kernel(xbc_ref, z_ref, dt_ref, dtt_ref, conv0_ref, h0_ref, cw_ref, cb_ref, dtb_ref, dtbt_ref,
                alog_ref, alogt_ref, dsk_ref, nw_ref, o_ref, hf_ref, xp_scr, st_scr, y_scr,
                *, n_heads, hp, n_groups, ns, n_valid):
    hi = lax.Precision.HIGHEST
    c = pl.program_id(1)
    q = xbc_ref.shape[0]
    width = n_heads * hp
    rep = n_heads // n_groups
    taps = cw_ref.shape[0]

    @pl.when(c == 0)
    def _():
        xp_scr[0:SUBLANES, :] = conv0_ref[...]
        st_scr[...] = h0_ref[...]

    xp_scr[SUBLANES:SUBLANES + q, :] = xbc_ref[...]
    conv = cb_ref[...]
    for w in range(taps):
        off = SUBLANES - (taps - 1) + w
        conv = conv + cw_ref[w:w + 1, :] * xp_scr[off:off + q, :]
    xp_scr[0:SUBLANES, :] = xp_scr[q:q + SUBLANES, :]
    xc = _silu(conv)
    xs = xc[:, :width]
    bm = xc[:, width:width + n_groups * ns]
    cm = xc[:, width + n_groups * ns:]

    row = lax.broadcasted_iota(jnp.int32, (q, q), 0)
    col = lax.broadcasted_iota(jnp.int32, (q, q), 1)
    tri = row >= col
    dt = _softplus(dt_ref[...] + dtb_ref[...])
    dtt = _softplus(dtt_ref[...] + dtbt_ref[...])
    if n_valid < q:
        dt = jnp.where(lax.broadcasted_iota(jnp.int32, dt.shape, 0) < n_valid, dt, 0.0)
        dtt = jnp.where(lax.broadcasted_iota(jnp.int32, dtt.shape, 1) < n_valid, dtt, 0.0)
    adt = dt * -jnp.exp(alog_ref[...])
    adtt = dtt * -jnp.exp(alogt_ref[...])
    cs = _dot(tri.astype(F32), adt, hi)
    cst = _dot(adtt, (row <= col).astype(F32), hi)
    cs_last = cs[q - 1:q, :]
    e_cs = jnp.exp(cs)
    dec = jnp.exp(cs_last - cs)
    e_last = jnp.exp(cs_last)

    for g in range(n_groups):
        bg = bm[:, g * ns:(g + 1) * ns]
        cg = cm[:, g * ns:(g + 1) * ns].astype(BF16)
        cb = lax.dot_general(cg, bg.astype(BF16), _NT, preferred_element_type=F32)
        for r in range(rep):
            h = g * rep + r
            x_h = xs[:, h * hp:(h + 1) * hp]
            xdt = (x_h * dt[:, h:h + 1]).astype(BF16)
            seg = cs[:, h:h + 1] - cst[h:h + 1, :]
            lm = jnp.exp(jnp.where(tri, seg, NEG_BIG))
            y = _dot((cb * lm).astype(BF16), xdt)
            s_prev = st_scr[h]
            y_off = lax.dot_general(cg, s_prev.astype(BF16), _NT, preferred_element_type=F32) * e_cs[:, h:h + 1]
            bw = (bg * dec[:, h:h + 1]).astype(BF16)
            st_scr[h] = e_last[:, h:h + 1] * s_prev + lax.dot_general(xdt, bw, _TN, preferred_element_type=F32)
            y_scr[:, h * hp:(h + 1) * hp] = y + y_off + dsk_ref[:, h * hp:(h + 1) * hp] * x_h

    yg = y_scr[...] * _silu(z_ref[...])
    o = yg * lax.rsqrt(jnp.mean(yg * yg, axis=-1, keepdims=True) + RMS_EPS) * nw_ref[...]
    o_ref[...] = o.astype(BF16)

    @pl.when(c == pl.num_programs(1) - 1)
    def _():
        hf_ref[...] = st_scr[...]


def _ssd(xbc, z, dt_raw, conv0, h0, params, lyr, n_seq, n_chunks, n_valid):
    conv_w, conv_b, dtb, dtbt, alog, alogt, dsk, nw, n_groups = params
    _, n_heads, hp, ns = h0.shape
    q = SSD_CHUNK
    ch, width = xbc.shape[1], z.shape[1]
    dtt = jnp.transpose(dt_raw[:, :n_heads].reshape(n_seq * n_chunks, q, n_heads), (0, 2, 1))
    rowblk = lambda wd: pl.BlockSpec((q, wd), lambda b, c: (b * n_chunks + c, 0))
    lyr_blk = lambda *shape: pl.BlockSpec((None,) + shape, lambda b, c: (lyr,) + (0,) * len(shape))
    seq_blk = lambda *shape: pl.BlockSpec((None,) + shape, lambda b, c: (b,) + (0,) * len(shape))
    return pl.pallas_call(
        functools.partial(_ssd_kernel, n_heads=n_heads, hp=hp, n_groups=n_groups, ns=ns, n_valid=n_valid),
        grid=(n_seq, n_chunks),
        in_specs=[rowblk(ch), rowblk(width), rowblk(LANES),
                  pl.BlockSpec((None, n_heads, q), lambda b, c: (b * n_chunks + c, 0, 0)),
                  seq_blk(SUBLANES, ch), seq_blk(n_heads, hp, ns),
                  lyr_blk(conv_w.shape[1], ch), lyr_blk(1, ch), lyr_blk(1, LANES), lyr_blk(n_heads, 1),
                  lyr_blk(1, LANES), lyr_blk(n_heads, 1), lyr_blk(1, width), lyr_blk(1, width)],
        out_specs=[rowblk(width), seq_blk(n_heads, hp, ns)],
        out_shape=[jax.ShapeDtypeStruct((n_seq * n_chunks * q, width), BF16),
                   jax.ShapeDtypeStruct((n_seq, n_heads, hp, ns), F32)],
        scratch_shapes=[pltpu.VMEM((q + SUBLANES, ch), F32), pltpu.VMEM((n_heads, hp, ns), F32),
                        pltpu.VMEM((q, width), F32)],
        compiler_params=_cparams("arbitrary", "arbitrary"),
    )(xbc, z, dt_raw, dtt, conv0, h0, conv_w, conv_b, dtb, dtbt, alog, alogt, dsk, nw)


def _pad_rows_per_seq(a, rows):
    n_seq, c = a.shape
    return jnp.pad(a[:, None, :], ((0, 0), (0, rows - 1), (0, 0))).reshape(n_seq * rows, c)


def _trunk(xp, xs, mods_p, mods_s, wts, dims, states):
    (w1, w3, w2, w_in, w_dt, w_gates, w_branch, w_out, w_glu, ln_g, ln_b, s5_params, s5_d,
     attn_sink, bias_p, bias_s, ssd_params) = wts
    depth, batch, seq, dec_b = dims["depth"], dims["batch"], dims["seq"], dims["dec_b"]
    n_kv, hd, wn = dims["n_kv"], dims["hd"], dims["window"]
    s5_w, ssd_w, taps, ch = dims["s5_w"], dims["ssd_w"], dims["taps"], dims["conv_ch"]
    ssd_heads, hp, ns = dims["ssd_heads"], dims["ssd_hp"], dims["ssd_state"]
    alpha = (2 * depth) ** 0.25
    kw = n_kv * hd
    qw = attn_sink.shape[1] * hd
    o_q = s5_w
    o_z = o_q + qw + 2 * kw
    o_xbc = o_z + ssd_w
    sq = SAMPLE_Q_ROWS

    p_out = {k: [] for k in ("s5_re", "s5_im", "ssm", "conv", "k", "v")}
    s_out = {k: [] for k in p_out}
    x = (xp, xs)
    mods = (mods_p, mods_s)
    h = tuple(_modulate(x[i], mods[i], 0) for i in range(2))

    def norm(x, y, ls, next_ls, coef):
        res = [_deepnorm(x[i], y[i], mods[i], ls, ln_g, ln_b, next_ls, alpha, coef) for i in range(2)]
        return (res[0][0], res[1][0]), (res[0][1], res[1][1])

    for lyr in range(depth):
        ls = 3 * lyr
        y = _mm(*_ffn_up(*h, w1, w3, 2 * lyr), w2, 2 * lyr)
        x, h = norm(x, y, ls, ls + 1, 0.5)

        proj = lambda off, n: _mm(*h, w_in, lyr, col_off=off, n=n, tn_cap=256)
        u = proj(0, s5_w)
        qkv = proj(o_q, qw + 2 * kw)
        z = proj(o_z, ssd_w)
        xbc = proj(o_xbc, ch)
        dt_raw = _mm(*h, w_dt, lyr)
        gates = _mm(*h, w_gates, lyr, act="sigmoid")

        a_p, re_p, im_p = _s5_prompt(u[0], _s5_prompt_operators(s5_params(lyr), s5_d[lyr]), batch, seq)
        a_s, re_s, im_s = _s5_sample(u[1], _s5_sample_operators(s5_params(lyr), s5_d[lyr]),
                                     states["s5_re"][lyr], states["s5_im"][lyr])
        for out, re, im in ((p_out, re_p, im_p), (s_out, re_s, im_s)):
            out["s5_re"].append(re)
            out["s5_im"].append(im)
        o_a = (_s5_glu(a_p, w_glu, lyr), _s5_glu(a_s, w_glu, lyr))

        n_blk = seq // wn
        cur = lambda b, n: b * n_blk + n
        prev = lambda b, n: b * n_blk + jnp.maximum(n - 1, 0)
        k_col, v_col = qw // kw, qw // kw + 1
        qkv_p, qkv_s = qkv
        ob_p = _swa(qkv_p, (qkv_p, prev, k_col), (qkv_p, cur, k_col), (qkv_p, prev, v_col), (qkv_p, cur, v_col),
                    bias_p, attn_sink[lyr], batch, n_blk, wn, n_kv, hd, first_has_prev=False)
        kv3 = qkv_p.reshape(batch, seq, qw + 2 * kw)
        p_out["k"].append(kv3[:, seq - wn:, qw:qw + kw].reshape(batch, wn, n_kv, hd))
        p_out["v"].append(kv3[:, seq - wn:, qw + kw:].reshape(batch, wn, n_kv, hd))
        one = lambda b, n: b
        k_new, v_new = qkv_s[:, qw:qw + kw], qkv_s[:, qw + kw:]
        k_cache = states["k"][lyr].reshape(dec_b * wn, kw)
        v_cache = states["v"][lyr].reshape(dec_b * wn, kw)
        ob_s = _swa(_pad_rows_per_seq(qkv_s[:, :qw], sq), (k_cache, one, 0), (_pad_rows_per_seq(k_new, wn), one, 0),
                    (v_cache, one, 0), (_pad_rows_per_seq(v_new, wn), one, 0),
                    bias_s, attn_sink[lyr], dec_b, 1, wn, n_kv, hd, first_has_prev=True)
        ob_s = ob_s.reshape(dec_b, sq, qw)[:, 0]
        s_out["k"].append(jnp.concatenate([states["k"][lyr][:, 1:], k_new.reshape(dec_b, 1, n_kv, hd)], axis=1))
        s_out["v"].append(jnp.concatenate([states["v"][lyr][:, 1:], v_new.reshape(dec_b, 1, n_kv, hd)], axis=1))

        cq = SSD_CHUNK
        oc_p, ssm_p = _ssd(xbc[0], z[0], dt_raw[0], jnp.zeros((batch, SUBLANES, ch), F32),
                           jnp.zeros((batch, ssd_heads, hp, ns), F32), ssd_params, lyr, batch, seq // cq, cq)
        p_out["conv"].append(xbc[0].reshape(batch, seq, ch)[:, seq - (taps - 1):])
        conv_hist = states["conv"][lyr]
        conv0 = jnp.pad(conv_hist, ((0, 0), (SUBLANES - (taps - 1), 0), (0, 0)))
        oc_s, ssm_s = _ssd(_pad_rows_per_seq(xbc[1], cq), _pad_rows_per_seq(z[1], cq), _pad_rows_per_seq(dt_raw[1], cq),
                           conv0, states["ssm"][lyr], ssd_params, lyr, dec_b, 1, 1)
        oc_s = oc_s.reshape(dec_b, cq, -1)[:, 0]
        s_out["conv"].append(jnp.concatenate([conv_hist[:, 1:], xbc[1][:, None, :]], axis=1))
        p_out["ssm"].append(ssm_p)
        s_out["ssm"].append(ssm_s)

        merged = _merge((o_a[0], ob_p, oc_p), (o_a[1], ob_s, oc_s), w_branch, lyr, *gates)
        y = _mm(*merged, w_out, lyr)
        x, h = norm(x, y, ls + 1, ls + 2, 1.0)

        y = _mm(*_ffn_up(*h, w1, w3, 2 * lyr + 1), w2, 2 * lyr + 1)
        x, h = norm(x, y, ls + 2, ls + 3 if lyr + 1 < depth else None, 0.5)

    names = ("s5_re", "s5_im", "ssm", "conv", "k", "v")
    return x, tuple(jnp.stack(p_out[k]) for k in names), tuple(jnp.stack(s_out[k]) for k in names)


def kernel(x_prompt, x_sample, state_s5_re, state_s5_im, state_ssm, state_conv, cache_k_win, cache_v_win,
           c_prompt, c_sample, w_ada, b_ada, ln_g, ln_b, ffn_w1, ffn_w3, ffn_w2, w_in, w_branch, w_out,
           s5_lam_re, s5_lam_im, s5_log_dt, s5_b_re, s5_b_im, s5_c_re, s5_c_im, s5_d, s5_w_glu,
           attn_sink, rel_bias, conv_w, conv_b, dt_bias, a_log, ssd_d, ssd_norm_w):
    batch, seq, d = x_prompt.shape
    dec_b, dec_seq, _ = x_sample.shape
    assert dec_seq == 1, "the sample group decodes one token per sequence"
    depth, n_sub = w_ada.shape[:2]
    d_ff = ffn_w1.shape[-1]
    g5, p5, i5 = s5_b_re.shape[1:]
    s5_w = g5 * i5
    n_heads = attn_sink.shape[1]
    _, _, wn, n_kv, hd = cache_k_win.shape
    qw, kw = n_heads * hd, n_kv * hd
    _, _, ssd_heads, hp, ns = state_ssm.shape
    ssd_w = ssd_heads * hp
    taps, ch = conv_w.shape[1:]
    n_groups = (ch - ssd_w) // (2 * ns)
    assert seq % wn == 0 and seq % SSD_CHUNK == 0 and seq % S5_CHUNK == 0
    assert ssd_heads <= LANES and taps - 1 <= SUBLANES and LANES % i5 == 0 and s5_w % LANES == 0

    bf = lambda w: w.astype(BF16)
    w1 = ffn_w1.reshape(depth * 2, d, d_ff)
    w3 = ffn_w3.reshape(depth * 2, d, d_ff)
    w2 = bf(ffn_w2).reshape(depth * 2, d_ff, d)
    o_dt = s5_w + qw + 2 * kw + ssd_w + ch
    o_g = o_dt + ssd_heads
    w_dt = jnp.pad(bf(w_in[:, :, o_dt:o_g]), ((0, 0), (0, 0), (0, LANES - ssd_heads)))
    w_gates = bf(w_in[:, :, o_g:])

    lane_pad = lambda v: jnp.pad(v, ((0, 0), (0, LANES - v.shape[1])))[:, None, :]
    ssd_params = (conv_w, conv_b[:, None, :], lane_pad(dt_bias), dt_bias[:, :, None],
                  lane_pad(a_log), a_log[:, :, None], jnp.repeat(ssd_d, hp, axis=1)[:, None, :],
                  ssd_norm_w[:, None, :], n_groups)
    s5_params = lambda lyr: (s5_lam_re[lyr], s5_lam_im[lyr], s5_log_dt[lyr], s5_b_re[lyr], s5_b_im[lyr],
                             s5_c_re[lyr], s5_c_im[lyr])
    bias_p = _rel_bias_table(rel_bias, wn, n_kv)
    rep = n_heads // n_kv
    bias_s = bias_p.reshape(n_kv, rep, wn, 2 * wn)[:, :, :SAMPLE_Q_ROWS].reshape(n_kv, rep * SAMPLE_Q_ROWS, 2 * wn)
    wts = (w1, w3, w2, w_in, w_dt, w_gates, bf(w_branch), bf(w_out), bf(s5_w_glu),
           ln_g.reshape(depth * n_sub, 1, d), ln_b.reshape(depth * n_sub, 1, d), s5_params, s5_d,
           attn_sink, bias_p, bias_s, ssd_params)

    rows = -(-(dec_b + batch) // SUBLANES) * SUBLANES
    c_all = jnp.pad(jnp.concatenate([c_sample, c_prompt], axis=0), ((0, rows - dec_b - batch), (0, 0)))
    mods = _adaln(c_all, w_ada.reshape(depth * n_sub, d, 3 * d), b_ada.reshape(depth * n_sub, 1, 3 * d))

    dims = dict(depth=depth, batch=batch, seq=seq, dec_b=dec_b, n_kv=n_kv, hd=hd, window=wn, s5_w=s5_w,
                ssd_w=ssd_w, ssd_heads=ssd_heads, ssd_hp=hp, ssd_state=ns, taps=taps, conv_ch=ch)
    states = dict(s5_re=state_s5_re, s5_im=state_s5_im, ssm=state_ssm, conv=state_conv, k=cache_k_win, v=cache_v_win)
    (yp, ys), p_state, s_state = _trunk(x_prompt.reshape(batch * seq, d), x_sample.reshape(dec_b, d),
                                        _Mods(mods, True, dec_b, seq, d), _Mods(mods, False, dec_b, 1, d),
                                        wts, dims, states)
    return (yp.reshape(batch, seq, d), ys.reshape(dec_b, 1, d)) + p_state + s_state
```

```python
import functools
import math

import jax
import jax.numpy as jnp
from jax import lax
from jax.experimental import pallas as pl
from jax.experimental.pallas import tpu as pltpu

F32 = jnp.float32
BF16 = jnp.bfloat16

LN_EPS = 1e-5
RMS_EPS = 1e-5
SSD_CHUNK = 128
S5_CHUNK = 16
LANES = 128
SUBLANES = 8
SAMPLE_Q_ROWS = 16
VMEM_BYTES_V7X = 64 * 1024 * 1024
VMEM_LIMIT = VMEM_BYTES_V7X - 8 * 1024 * 1024
NEG_BIG = -1e30

_NT = (((1,), (1,)), ((), ()))
_TN = (((0,), (0,)), ((), ()))


def _pick_tile(n, cap, align):
    if n <= cap:
        return n
    t = (cap // align) * align
    while t >= align:
        if n % t == 0:
            return t
        t -= align
    raise ValueError(f"no tile for {n} under {cap} aligned to {align}")


def _cparams(*sem):
    return pltpu.CompilerParams(dimension_semantics=sem, vmem_limit_bytes=VMEM_LIMIT)


def _sigmoid(x):
    return 1.0 / (1.0 + jnp.exp(-x))


def _silu(x):
    return x * _sigmoid(x)


def _softplus(x):
    return jnp.maximum(x, 0.0) + jnp.log1p(jnp.exp(-jnp.abs(x)))


def _dot(a, b, precision=None):
    return jnp.dot(a, b, preferred_element_type=F32, precision=precision)


def _adaln_kernel(c_ref, w_ref, b_ref, o_ref):
    a = _silu(c_ref[...]).astype(BF16)
    o_ref[...] = _dot(a, w_ref[...].astype(BF16)) + b_ref[...]


def _adaln(c_all, w_ada, b_ada):
    n_ls, d, n3 = w_ada.shape
    rows = c_all.shape[0]
    tn = _pick_tile(n3, 512, LANES)
    return pl.pallas_call(
        _adaln_kernel,
        grid=(n_ls, n3 // tn),
        in_specs=[pl.BlockSpec((rows, d), lambda s, j: (0, 0)),
                  pl.BlockSpec((None, d, tn), lambda s, j: (s, 0, j)),
                  pl.BlockSpec((None, 1, tn), lambda s, j: (s, 0, j))],
        out_specs=pl.BlockSpec((None, rows, tn), lambda s, j: (s, 0, j)),
        out_shape=jax.ShapeDtypeStruct((n_ls, rows, n3), F32),
        compiler_params=_cparams("arbitrary", "arbitrary"),
    )(c_all, w_ada, b_ada)


class _Mods:
    def __init__(self, mods, prompt, dec_b, seq, d):
        self.prompt, self.dec_b, self.seq, self.d = prompt, dec_b, seq, d
        n_ls, rows, n3 = mods.shape
        self.arr = mods.reshape(n_ls, rows, 1, n3) if prompt else mods

    def spec(self, ls, part, tm):
        d = self.d
        if self.prompt:
            dec_b, seq = self.dec_b, self.seq
            return pl.BlockSpec((None, None, 1, d), lambda i, *_: (ls, dec_b + (i * tm) // seq, 0, part))
        return pl.BlockSpec((None, self.dec_b, d), lambda i, *_: (ls, 0, part))


def _modulate_kernel(x_ref, sh_ref, sc_ref, h_ref):
    h_ref[...] = (x_ref[...] * (1.0 + sc_ref[...]) + sh_ref[...]).astype(BF16)


def _modulate(x, mods, ls):
    m, d = x.shape
    tm = _pick_tile(mods.seq, 256, 16) if mods.prompt else m
    return pl.pallas_call(
        _modulate_kernel,
        grid=(m // tm,),
        in_specs=[pl.BlockSpec((tm, d), lambda i: (i, 0)), mods.spec(ls, 0, tm), mods.spec(ls, 1, tm)],
        out_specs=pl.BlockSpec((tm, d), lambda i: (i, 0)),
        out_shape=jax.ShapeDtypeStruct((m, d), BF16),
        compiler_params=_cparams("arbitrary"),
    )(x, mods.arr, mods.arr)


def _deepnorm_kernel(x_ref, y_ref, gate_ref, g_ref, b_ref, *rest, alpha, coef, with_h):
    t = alpha * x_ref[...] + (1.0 + gate_ref[...]) * (coef * y_ref[...])
    mu = jnp.mean(t, axis=-1, keepdims=True)
    dlt = t - mu
    var = jnp.mean(dlt * dlt, axis=-1, keepdims=True)
    xn = dlt * lax.rsqrt(var + LN_EPS) * g_ref[...] + b_ref[...]
    if with_h:
        sh_ref, sc_ref, xo_ref, ho_ref = rest
        ho_ref[...] = (xn * (1.0 + sc_ref[...]) + sh_ref[...]).astype(BF16)
    else:
        (xo_ref,) = rest
    xo_ref[...] = xn


def _deepnorm(x, y, mods, ls, ln_g, ln_b, next_ls, alpha, coef):
    m, d = x.shape
    tm = _pick_tile(mods.seq, 256, 16) if mods.prompt else m
    with_h = next_ls is not None
    row = pl.BlockSpec((tm, d), lambda i: (i, 0))
    vec = pl.BlockSpec((None, 1, d), lambda i: (ls, 0, 0))
    in_specs = [row, row, mods.spec(ls, 2, tm), vec, vec]
    args = [x, y, mods.arr, ln_g, ln_b]
    out_specs, out_shape = [row], [jax.ShapeDtypeStruct((m, d), F32)]
    if with_h:
        in_specs += [mods.spec(next_ls, 0, tm), mods.spec(next_ls, 1, tm)]
        args += [mods.arr, mods.arr]
        out_specs.append(row)
        out_shape.append(jax.ShapeDtypeStruct((m, d), BF16))
    res = pl.pallas_call(
        functools.partial(_deepnorm_kernel, alpha=alpha, coef=coef, with_h=with_h),
        grid=(m // tm,), in_specs=in_specs, out_specs=out_specs, out_shape=out_shape,
        compiler_params=_cparams("arbitrary"),
    )(*args)
    return (res[0], res[1]) if with_h else (res[0], None)


def _col_slab(j, tn):
    return pl.ds(pl.multiple_of(j * tn, tn), tn)


def _mm_kernel(x_ref, xs_ref, w_ref, o_ref, os_ref, *, act, tn):
    def apply(x):
        r = _dot(x, w_ref[...].astype(BF16))
        return _sigmoid(r) if act == "sigmoid" else r

    o_ref[...] = apply(x_ref[...]).astype(o_ref.dtype)

    @pl.when(pl.program_id(0) == 0)
    def _():
        os_ref[:, _col_slab(pl.program_id(1), tn)] = apply(xs_ref[...]).astype(os_ref.dtype)


def _row_tile(m, k):
    cap = 2048 if k <= 4096 else 512
    return _pick_tile(m, cap, 16)


def _row_spec(tm, k, single):
    mode = dict(pipeline_mode=pl.Buffered(1)) if single else {}
    return pl.BlockSpec((tm, k), lambda i, j: (i, 0), **mode)


def _full_spec(shape):
    return pl.BlockSpec(shape, lambda i, j: (0,) * len(shape))


def _mm(x, xs, w, widx, col_off=0, n=None, tn_cap=512, act=None, out_dtype=F32):
    m, k = x.shape
    n = w.shape[-1] if n is None else n
    tm, tn = _row_tile(m, k), _pick_tile(math.gcd(n, col_off) if col_off else n, tn_cap, LANES)
    col0 = col_off // tn
    single = tm * k * 2 > 12 * 1024 * 1024
    ms = xs.shape[0]
    return pl.pallas_call(
        functools.partial(_mm_kernel, act=act, tn=tn),
        grid=(m // tm, n // tn),
        in_specs=[_row_spec(tm, k, single), _full_spec((ms, k)),
                  pl.BlockSpec((None, k, tn), lambda i, j: (widx, 0, col0 + j))],
        out_specs=[pl.BlockSpec((tm, tn), lambda i, j: (i, j)), _full_spec((ms, n))],
        out_shape=[jax.ShapeDtypeStruct((m, n), out_dtype), jax.ShapeDtypeStruct((ms, n), out_dtype)],
        compiler_params=_cparams("arbitrary", "arbitrary"),
    )(x, xs, w)


def _ffn_up_kernel(h_ref, hs_ref, w1_ref, w3_ref, o_ref, os_ref, *, tn):
    def apply(h):
        a = _dot(h, w1_ref[...].astype(BF16))
        return (_silu(a) * _dot(h, w3_ref[...].astype(BF16))).astype(BF16)

    o_ref[...] = apply(h_ref[...])

    @pl.when(pl.program_id(0) == 0)
    def _():
        os_ref[:, _col_slab(pl.program_id(1), tn)] = apply(hs_ref[...])


def _ffn_up(h, hs, w1, w3, widx):
    m, k = h.shape
    n = w1.shape[-1]
    tm, tn = _row_tile(m, k), _pick_tile(n, 256, LANES)
    ms = hs.shape[0]
    wspec = pl.BlockSpec((None, k, tn), lambda i, j: (widx, 0, j))
    return pl.pallas_call(
        functools.partial(_ffn_up_kernel, tn=tn),
        grid=(m // tm, n // tn),
        in_specs=[_row_spec(tm, k, tm * k * 2 > 12 * 1024 * 1024), _full_spec((ms, k)), wspec, wspec],
        out_specs=[pl.BlockSpec((tm, tn), lambda i, j: (i, j)), _full_spec((ms, n))],
        out_shape=[jax.ShapeDtypeStruct((m, n), BF16), jax.ShapeDtypeStruct((ms, n), BF16)],
        compiler_params=_cparams("arbitrary", "arbitrary"),
    )(h, hs, w1, w3)


def _merge_kernel(oa_ref, ob_ref, oc_ref, sa_ref, sb_ref, sc_ref, wa_ref, wb_ref, wc_ref,
                  ga_ref, gb_ref, gc_ref, gs_ref, o_ref, os_ref, *, tn, d):
    def apply(ga, gb, gc, a, b, c):
        return (ga * _dot(a, wa_ref[...]) + gb * _dot(b, wb_ref[...]) + gc * _dot(c, wc_ref[...])).astype(BF16)

    o_ref[...] = apply(ga_ref[...], gb_ref[...], gc_ref[...], oa_ref[...], ob_ref[...], oc_ref[...])

    @pl.when(pl.program_id(0) == 0)
    def _():
        j = pl.program_id(1)
        gate = lambda part: gs_ref[:, pl.ds(pl.multiple_of(part * d + j * tn, tn), tn)]
        os_ref[:, _col_slab(j, tn)] = apply(gate(0), gate(1), gate(2), sa_ref[...], sb_ref[...], sc_ref[...])


def _merge(branches_p, branches_s, w_branch, lyr, gates_p, gates_s):
    o_a, o_b, o_c = branches_p
    m, ka = o_a.shape
    kb, kc = o_b.shape[1], o_c.shape[1]
    d = w_branch.shape[-1]
    ms = gates_s.shape[0]
    assert ka == kb and kc % (ka + kb) == 0, "branch widths must tile the stacked branch weight"
    tm, tn = _pick_tile(m, 1024, 16), _pick_tile(d, 512, LANES)
    nj = d // tn
    rows = lambda width: pl.BlockSpec((tm, width), lambda i, j: (i, 0))
    return pl.pallas_call(
        functools.partial(_merge_kernel, tn=tn, d=d),
        grid=(m // tm, nj),
        in_specs=[rows(ka), rows(kb), rows(kc),
                  _full_spec((ms, ka)), _full_spec((ms, kb)), _full_spec((ms, kc)),
                  pl.BlockSpec((None, ka, tn), lambda i, j: (lyr, 0, j)),
                  pl.BlockSpec((None, kb, tn), lambda i, j: (lyr, 1, j)),
                  pl.BlockSpec((None, kc, tn), lambda i, j: (lyr, (ka + kb) // kc, j)),
                  pl.BlockSpec((tm, tn), lambda i, j: (i, j)),
                  pl.BlockSpec((tm, tn), lambda i, j: (i, nj + j)),
                  pl.BlockSpec((tm, tn), lambda i, j: (i, 2 * nj + j)),
                  _full_spec((ms, 3 * d))],
        out_specs=[pl.BlockSpec((tm, tn), lambda i, j: (i, j)), _full_spec((ms, d))],
        out_shape=[jax.ShapeDtypeStruct((m, d), BF16), jax.ShapeDtypeStruct((ms, d), BF16)],
        compiler_params=_cparams("arbitrary", "arbitrary"),
    )(o_a, o_b, o_c, *branches_s, w_branch, w_branch, w_branch, gates_p, gates_p, gates_p, gates_s)


def _cmul(ar, ai, br, bi):
    return ar * br - ai * bi, ar * bi + ai * br


def _gelu_tanh(x):
    return jax.nn.gelu(x, approximate=True)


def _s5_tile_kernel(u_ref, kd_ref, bd_ref, cd_ref, apr_ref, api_ref, d_ref, a_ref, hr_ref, hi_ref,
                    w_scr, shi_scr, slo_scr, e_scr, hp_scr, y_scr, *, q, n_chunks, nb):
    lt = LANES
    sw = bd_ref.shape[-1] // 2
    zero = jnp.zeros((lt, lt), BF16)
    for s in range(q):
        for l in range(q):
            w_scr[s * lt:(s + 1) * lt, l * lt:(l + 1) * lt] = kd_ref[l - s] if l >= s else zero
    b_re, b_im = bd_ref[:, :sw], bd_ref[:, sw:]
    for s in range(q):
        s_re, s_im = _cmul(b_re, b_im, apr_ref[q - 1 - s:q - s, :], api_ref[q - 1 - s:q - s, :])
        s_full = jnp.concatenate([s_re, s_im], axis=1)
        s_hi = s_full.astype(BF16)
        shi_scr[s * lt:(s + 1) * lt, :] = s_hi
        slo_scr[s * lt:(s + 1) * lt, :] = (s_full - s_hi.astype(F32)).astype(BF16)
    u = u_ref[...]
    u_hi = u.astype(BF16)
    u_lo = (u - u_hi.astype(F32)).astype(BF16)
    e_scr[...] = _dot(u_hi, shi_scr[...]) + _dot(u_lo, shi_scr[...]) + _dot(u_hi, slo_scr[...])
    aq_re, aq_im = apr_ref[q:q + 1, :], api_ref[q:q + 1, :]
    h_re = jnp.zeros((nb, sw), F32)
    h_im = jnp.zeros((nb, sw), F32)
    for c in range(n_chunks):
        rows = slice(c * nb, (c + 1) * nb)
        hp_scr[rows, :sw] = h_re
        hp_scr[rows, sw:] = h_im
        t_re, t_im = _cmul(h_re, h_im, aq_re, aq_im)
        h_re, h_im = t_re + e_scr[rows, :sw], t_im + e_scr[rows, sw:]
    hr_ref[...] = h_re
    hi_ref[...] = h_im
    y_scr[...] = _dot(u_hi, w_scr[...])
    p_re, p_im = hp_scr[:, :sw], hp_scr[:, sw:]
    cd = cd_ref[...]
    for l in range(q):
        t_re, t_im = _cmul(p_re, p_im, apr_ref[l + 1:l + 2, :], api_ref[l + 1:l + 2, :])
        adv = jnp.concatenate([t_re, t_im], axis=1).astype(BF16)
        y_scr[:, l * lt:(l + 1) * lt] += _dot(adv, cd)
    a_ref[...] = _gelu_tanh(y_scr[...] + d_ref[...] * u).astype(BF16)


def _s5_prompt(u, ops, n_seq, seq):
    kd, bd, cd, ap_re, ap_im, d_t, gpt, p = ops
    q = S5_CHUNK
    n_t, n_c, width = kd.shape[1], seq // q, u.shape[1]
    rows, cols, sw = n_c * n_seq, q * LANES, bd.shape[-1] // 2
    u_t = jnp.transpose(u.reshape(n_seq, n_c, q, n_t, LANES), (3, 1, 0, 2, 4)).reshape(n_t, rows, cols)
    blk = lambda *shape: pl.BlockSpec((None,) + shape, lambda i: (i,) + (0,) * len(shape))
    a_t, h_re, h_im = pl.pallas_call(
        functools.partial(_s5_tile_kernel, q=q, n_chunks=n_c, nb=n_seq),
        grid=(n_t,),
        in_specs=[blk(rows, cols), pl.BlockSpec((q, None, LANES, LANES), lambda i: (0, i, 0, 0)),
                  blk(LANES, 2 * sw), blk(2 * sw, LANES), blk(q + 1, sw), blk(q + 1, sw), blk(1, cols)],
        out_specs=[blk(rows, cols), blk(n_seq, sw), blk(n_seq, sw)],
        out_shape=[jax.ShapeDtypeStruct((n_t, rows, cols), BF16),
                   jax.ShapeDtypeStruct((n_t, n_seq, sw), F32), jax.ShapeDtypeStruct((n_t, n_seq, sw), F32)],
        scratch_shapes=[pltpu.VMEM((cols, cols), BF16), pltpu.VMEM((cols, 2 * sw), BF16),
                        pltpu.VMEM((cols, 2 * sw), BF16), pltpu.VMEM((rows, 2 * sw), F32),
                        pltpu.VMEM((rows, 2 * sw), F32), pltpu.VMEM((rows, cols), F32)],
        compiler_params=_cparams("arbitrary"),
    )(u_t, kd, bd, cd, ap_re, ap_im, d_t)
    a = jnp.transpose(a_t.reshape(n_t, n_c, n_seq, q, LANES), (2, 1, 3, 0, 4)).reshape(n_seq * seq, width)
    to_state = lambda h: jnp.transpose(h.reshape(n_t, n_seq, gpt, p), (1, 0, 2, 3)).reshape(n_seq, n_t * gpt, p)
    return a, to_state(h_re), to_state(h_im)


def _s5_step_kernel(u_ref, st_ref, tt_ref, rt_ref, a1_ref, a2_ref, d_ref, h0_ref, a_ref, hf_ref):
    hi = lax.Precision.HIGHEST
    u, h0 = u_ref[...], h0_ref[...]
    half = h0.shape[-1] // 2
    hf_ref[...] = h0 * a1_ref[...] + pltpu.roll(h0, half, axis=1) * a2_ref[...] + _dot(u, st_ref[...], hi)
    y = _dot(u, tt_ref[...], hi) + _dot(h0, rt_ref[...], hi)
    a_ref[...] = _gelu_tanh(y + d_ref[...] * u).astype(BF16)


def _s5_sample(u, ops, s5_re, s5_im):
    st, tt, rt, a1, a2, d_g = ops
    g, kq, p2 = st.shape
    nb, i_ch = u.shape[0], u.shape[1] // g
    u_t = jnp.pad(jnp.transpose(u.reshape(nb, g, i_ch), (1, 0, 2)), ((0, 0), (0, 0), (0, kq - i_ch)))
    h0 = jnp.concatenate([jnp.transpose(s5_re, (1, 0, 2)), jnp.transpose(s5_im, (1, 0, 2))], axis=-1)
    blk = lambda *shape: pl.BlockSpec((None,) + shape, lambda i: (i,) + (0,) * len(shape))
    a_t, hf = pl.pallas_call(
        _s5_step_kernel,
        grid=(g,),
        in_specs=[blk(nb, kq), blk(kq, p2), blk(kq, kq), blk(p2, kq), blk(1, p2), blk(1, p2), blk(1, kq), blk(nb, p2)],
        out_specs=[blk(nb, kq), blk(nb, p2)],
        out_shape=[jax.ShapeDtypeStruct((g, nb, kq), BF16), jax.ShapeDtypeStruct((g, nb, p2), F32)],
        compiler_params=_cparams("arbitrary"),
    )(u_t, st, tt, rt, a1, a2, d_g, h0)
    a = jnp.transpose(a_t[:, :, :i_ch], (1, 0, 2)).reshape(nb, g * i_ch)
    half = p2 // 2
    return a, jnp.transpose(hf[:, :, :half], (1, 0, 2)), jnp.transpose(hf[:, :, half:], (1, 0, 2))


def _s5_basis(lam_re, lam_im, log_dt, b_re, b_im, c_re, c_im, q):
    hi = lax.Precision.HIGHEST
    dt = jnp.exp(log_dt)[:, None]
    t = jnp.arange(q + 1, dtype=F32)[:, None, None]
    mag = jnp.exp(lam_re[None] * dt[None] * t)
    ang = lam_im[None] * dt[None] * t
    ap_re, ap_im = mag * jnp.cos(ang), mag * jnp.sin(ang)
    nr, ni = ap_re[1] - 1.0, ap_im[1]
    den = lam_re * lam_re + lam_im * lam_im
    f_re, f_im = (nr * lam_re + ni * lam_im) / den, (ni * lam_re - nr * lam_im) / den
    bb_re, bb_im = _cmul(f_re[..., None], f_im[..., None], b_re, b_im)
    ab_re, ab_im = _cmul(ap_re[:q, :, :, None], ap_im[:q, :, :, None], bb_re[None], bb_im[None])
    k_t = (jnp.einsum("gip,tgpj->tgij", c_re, ab_re, precision=hi)
           - jnp.einsum("gip,tgpj->tgij", c_im, ab_im, precision=hi))
    return ap_re, ap_im, bb_re, bb_im, k_t


def _s5_prompt_operators(params, d_skip):
    lam_re, lam_im, log_dt, b_re, b_im, c_re, c_im = params
    q = S5_CHUNK
    g, p, i_ch = b_re.shape
    gpt = LANES // i_ch
    n_t = g // gpt
    ap_re, ap_im, bb_re, bb_im, k_t = _s5_basis(*params, q)
    eye = jnp.eye(gpt, dtype=F32)
    kd = jnp.einsum("tqgij,gh->tqgjhi", k_t.reshape(q, n_t, gpt, i_ch, i_ch), eye).reshape(q, n_t, LANES, LANES)
    to_bd = lambda v: jnp.einsum("qgpj,gh->qgjhp", v.reshape(n_t, gpt, p, i_ch), eye).reshape(n_t, LANES, gpt * p)
    bd = jnp.concatenate([to_bd(bb_re), to_bd(bb_im)], axis=-1)
    to_cd = lambda v: jnp.einsum("qgip,gh->qgphi", v.reshape(n_t, gpt, i_ch, p), eye).reshape(n_t, gpt * p, LANES)
    cd = jnp.concatenate([to_cd(c_re), -to_cd(c_im)], axis=1)
    to_t = lambda v: jnp.transpose(v.reshape(q + 1, n_t, gpt * p), (1, 0, 2))
    d_t = jnp.tile(d_skip.reshape(n_t, 1, LANES), (1, 1, q))
    return kd.astype(BF16), bd, cd.astype(BF16), to_t(ap_re), to_t(ap_im), d_t, gpt, p


def _s5_sample_operators(params, d_skip):
    ap_re, ap_im, bb_re, bb_im, k_t = _s5_basis(*params, 1)
    c_re, c_im = params[5], params[6]
    g, p, i_ch = bb_re.shape
    pad = LANES - i_ch
    st = jnp.pad(jnp.concatenate([jnp.transpose(bb_re, (0, 2, 1)), jnp.transpose(bb_im, (0, 2, 1))], axis=-1),
                 ((0, 0), (0, pad), (0, 0)))
    tt = jnp.pad(jnp.transpose(k_t[0], (0, 2, 1)), ((0, 0), (0, pad), (0, pad)))
    rc_re, rc_im = _cmul(c_re, c_im, ap_re[1][:, None, :], ap_im[1][:, None, :])
    rt = jnp.pad(jnp.concatenate([jnp.transpose(rc_re, (0, 2, 1)), -jnp.transpose(rc_im, (0, 2, 1))], axis=1),
                 ((0, 0), (0, 0), (0, pad)))
    a1 = jnp.concatenate([ap_re[1], ap_re[1]], axis=-1)[:, None, :]
    a2 = jnp.concatenate([-ap_im[1], ap_im[1]], axis=-1)[:, None, :]
    d_g = jnp.pad(d_skip.reshape(g, 1, i_ch), ((0, 0), (0, 0), (0, pad)))
    return st, tt, rt, a1, a2, d_g


def _s5_glu_kernel(a_ref, w_ref, o_ref):
    r = _dot(a_ref[...], w_ref[...])
    wd = o_ref.shape[-1]
    o_ref[...] = (r[:, :wd] * _sigmoid(r[:, wd:])).astype(BF16)


def _s5_glu(a, w_glu, lyr):
    m, wd = a.shape
    tm = _pick_tile(m, 1024, 16)
    row = pl.BlockSpec((tm, wd), lambda i: (i, 0))
    return pl.pallas_call(
        _s5_glu_kernel,
        grid=(m // tm,),
        in_specs=[row, pl.BlockSpec((None, wd, 2 * wd), lambda i: (lyr, 0, 0))],
        out_specs=row,
        out_shape=jax.ShapeDtypeStruct((m, wd), BF16),
        compiler_params=_cparams("arbitrary"),
    )(a, w_glu)


def _swa_kernel(sink_ref, q_ref, kp_ref, kc_ref, vp_ref, vc_ref, bias_ref, o_ref,
                *, n_kv, rep, hd, first_has_prev, scale):
    n = pl.program_id(1)
    wq, wn = q_ref.shape[0], kc_ref.shape[0]
    q = q_ref[...]
    k = jnp.concatenate([kp_ref[...], kc_ref[...]], axis=0).astype(BF16)
    v = jnp.concatenate([vp_ref[...], vc_ref[...]], axis=0).astype(BF16)
    l_i = lax.broadcasted_iota(jnp.int32, (rep, wq, 2 * wn), 1).reshape(rep * wq, 2 * wn)
    k_j = lax.broadcasted_iota(jnp.int32, (rep * wq, 2 * wn), 1)
    dist = l_i + wn - k_j
    valid = (dist >= 0) & (dist < wn)
    if not first_has_prev:
        valid = valid & ((n > 0) | (k_j >= wn))
    for h in range(n_kv):
        heads = [h * rep + r for r in range(rep)]
        qs = jnp.concatenate([q[:, a * hd:(a + 1) * hd] for a in heads], axis=0).astype(BF16)
        kh, vh = k[:, h * hd:(h + 1) * hd], v[:, h * hd:(h + 1) * hd]
        s = lax.dot_general(qs, kh, _NT, preferred_element_type=F32) * scale + bias_ref[h]
        s = jnp.where(valid, s, NEG_BIG)
        sink = jnp.concatenate([jnp.full((wq, 1), sink_ref[a], F32) for a in heads], axis=0)
        mx = jnp.maximum(jnp.max(s, axis=1, keepdims=True), sink)
        p = jnp.exp(s - mx)
        den = jnp.sum(p, axis=1, keepdims=True) + jnp.exp(sink - mx)
        o = _dot(p.astype(BF16), vh) / den
        for r, a in enumerate(heads):
            o_ref[:, a * hd:(a + 1) * hd] = o[r * wq:(r + 1) * wq].astype(BF16)


def _swa(q_arr, kp, kc, vp, vc, bias, sink, n_seq, n_blk, wn, n_kv, hd, first_has_prev):
    n_heads = sink.shape[0]
    rep = n_heads // n_kv
    wq = bias.shape[1] // rep
    qw, kw = n_heads * hd, n_kv * hd

    def kv_spec(src):
        _, rmap, col = src
        return pl.BlockSpec((wn, kw), lambda b, n: (rmap(b, n), col))

    return pl.pallas_call(
        functools.partial(_swa_kernel, n_kv=n_kv, rep=rep, hd=hd, first_has_prev=first_has_prev,
                          scale=hd ** -0.5),
        grid=(n_seq, n_blk),
        in_specs=[pl.BlockSpec(memory_space=pltpu.SMEM),
                  pl.BlockSpec((wq, qw), lambda b, n: (b * n_blk + n, 0)),
                  kv_spec(kp), kv_spec(kc), kv_spec(vp), kv_spec(vc),
                  pl.BlockSpec(bias.shape, lambda b, n: (0, 0, 0))],
        out_specs=pl.BlockSpec((wq, qw), lambda b, n: (b * n_blk + n, 0)),
        out_shape=jax.ShapeDtypeStruct((n_seq * n_blk * wq, qw), BF16),
        compiler_params=_cparams("arbitrary", "arbitrary"),
    )(sink, q_arr, kp[0], kc[0], vp[0], vc[0], bias)


def _rel_bias_table(rel_bias, wn, n_kv):
    n_buckets, n_heads = rel_bias.shape
    max_exact = n_buckets // 2
    dist = (jnp.arange(wn)[:, None] + wn) - jnp.arange(2 * wn)[None, :]
    dist = jnp.clip(dist, 0, wn - 1)
    d = jnp.maximum(dist, 1).astype(F32)
    large = max_exact + (jnp.log(d / max_exact) / math.log(wn / max_exact)
                         * (n_buckets - max_exact)).astype(jnp.int32)
    bucket = jnp.where(dist < max_exact, dist, jnp.minimum(large, n_buckets - 1))
    bias = jnp.transpose(rel_bias[bucket], (2, 0, 1)).astype(F32)
    return bias.reshape(n_kv, (n_heads // n_kv) * wn, 2 * wn)


def _ssd_kernel(xbc_ref, z_ref, dt_ref, dtt_ref, conv0_ref, h0_ref, cw_ref, cb_ref, dtb_ref, dtbt_ref,
                alog_ref, alogt_ref, dsk_ref, nw_ref, ex_ref, o_ref, hf_ref, xp_scr, st_scr, y_scr, yoff_scr,
                *, n_heads, hp, n_groups, ns):
    hi = lax.Precision.HIGHEST
    c = pl.program_id(1)
    q = xbc_ref.shape[0]
    width = n_heads * hp
    rep = n_heads // n_groups
    taps = cw_ref.shape[0]

    @pl.when(c == 0)
    def _():
        xp_scr[0:SUBLANES, :] = conv0_ref[...]
        st_scr[...] = h0_ref[...]

    xp_scr[SUBLANES:SUBLANES + q, :] = xbc_ref[...]
    conv = cb_ref[...]
    for w in range(taps):
        off = SUBLANES - (taps - 1) + w
        conv = conv + cw_ref[w:w + 1, :] * xp_scr[off:off + q, :]
    xp_scr[0:SUBLANES, :] = xp_scr[q:q + SUBLANES, :]
    xc = _silu(conv)
    xs = xc[:, :width]
    bm = xc[:, width:width + n_groups * ns]
    cm = xc[:, width + n_groups * ns:]

    row = lax.broadcasted_iota(jnp.int32, (q, q), 0)
    col = lax.broadcasted_iota(jnp.int32, (q, q), 1)
    tri = row >= col
    dt = _softplus(dt_ref[...] + dtb_ref[...])
    dtt = _softplus(dtt_ref[...] + dtbt_ref[...])
    adt = dt * -jnp.exp(alog_ref[...])
    adtt = dtt * -jnp.exp(alogt_ref[...])
    cs = _dot(tri.astype(F32), adt, hi)
    cst = _dot(adtt, (row <= col).astype(F32), hi)
    cs_last = cs[q - 1:q, :]
    e_cs = jnp.exp(cs)
    dec = jnp.exp(cs_last - cs)
    e_last = jnp.exp(cs_last)

    def expand(v):
        hi_ = v.astype(BF16)
        r1 = v - hi_.astype(F32)
        mid = r1.astype(BF16)
        lo = (r1 - mid.astype(F32)).astype(BF16)
        return _dot(hi_, ex_ref[...]) + _dot(mid, ex_ref[...]) + _dot(lo, ex_ref[...])

    dt_x, ecs_x, dec_x = expand(dt), expand(e_cs), expand(dec)
    xdt_all = xs * dt_x
    xdt16 = xdt_all.astype(BF16)
    xdd16 = (xdt_all * dec_x).astype(BF16)
    for g in range(n_groups):
        bg = bm[:, g * ns:(g + 1) * ns].astype(BF16)
        cg = cm[:, g * ns:(g + 1) * ns].astype(BF16)
        cb = lax.dot_general(cg, bg, _NT, preferred_element_type=F32)
        for r in range(rep):
            h = g * rep + r
            cols = slice(h * hp, (h + 1) * hp)
            seg = cs[:, h:h + 1] - cst[h:h + 1, :]
            lm = jnp.exp(jnp.where(tri, seg, NEG_BIG))
            y_scr[:, cols] = _dot((cb * lm).astype(BF16), xdt16[:, cols])
            s_prev = st_scr[h]
            yoff_scr[:, cols] = lax.dot_general(cg, s_prev.astype(BF16), _NT, preferred_element_type=F32)
            st_scr[h] = e_last[:, h:h + 1] * s_prev + lax.dot_general(xdd16[:, cols], bg, _TN, preferred_element_type=F32)

    yg = (y_scr[...] + yoff_scr[...] * ecs_x + dsk_ref[...] * xs) * _silu(z_ref[...])
    o = yg * lax.rsqrt(jnp.mean(yg * yg, axis=-1, keepdims=True) + RMS_EPS) * nw_ref[...]
    o_ref[...] = o.astype(BF16)

    @pl.when(c == pl.num_programs(1) - 1)
    def _():
        hf_ref[...] = st_scr[...]


def _ssd(xbc, z, dt_raw, conv0, h0, params, lyr, n_seq, n_chunks):
    conv_w, conv_b, dtb, dtbt, alog, alogt, dsk, nw, n_groups = params
    _, n_heads, hp, ns = h0.shape
    q = SSD_CHUNK
    ch, width = xbc.shape[1], z.shape[1]
    dtt = jnp.transpose(dt_raw[:, :n_heads].reshape(n_seq * n_chunks, q, n_heads), (0, 2, 1))
    head_lanes = (jnp.arange(LANES)[:, None] == jnp.arange(width)[None, :] // hp).astype(BF16)
    rowblk = lambda wd: pl.BlockSpec((q, wd), lambda b, c: (b * n_chunks + c, 0))
    lyr_blk = lambda *shape: pl.BlockSpec((None,) + shape, lambda b, c: (lyr,) + (0,) * len(shape))
    seq_blk = lambda *shape: pl.BlockSpec((None,) + shape, lambda b, c: (b,) + (0,) * len(shape))
    return pl.pallas_call(
        functools.partial(_ssd_kernel, n_heads=n_heads, hp=hp, n_groups=n_groups, ns=ns),
        grid=(n_seq, n_chunks),
        in_specs=[rowblk(ch), rowblk(width), rowblk(LANES),
                  pl.BlockSpec((None, n_heads, q), lambda b, c: (b * n_chunks + c, 0, 0)),
                  seq_blk(SUBLANES, ch), seq_blk(n_heads, hp, ns),
                  lyr_blk(conv_w.shape[1], ch), lyr_blk(1, ch), lyr_blk(1, LANES), lyr_blk(n_heads, 1),
                  lyr_blk(1, LANES), lyr_blk(n_heads, 1), lyr_blk(1, width), lyr_blk(1, width),
                  pl.BlockSpec((LANES, width), lambda b, c: (0, 0))],
        out_specs=[rowblk(width), seq_blk(n_heads, hp, ns)],
        out_shape=[jax.ShapeDtypeStruct((n_seq * n_chunks * q, width), BF16),
                   jax.ShapeDtypeStruct((n_seq, n_heads, hp, ns), F32)],
        scratch_shapes=[pltpu.VMEM((q + SUBLANES, ch), F32), pltpu.VMEM((n_heads, hp, ns), F32),
                        pltpu.VMEM((q, width), F32), pltpu.VMEM((q, width), F32)],
        compiler_params=_cparams("arbitrary", "arbitrary"),
    )(xbc, z, dt_raw, dtt, conv0, h0, conv_w, conv_b, dtb, dtbt, alog, alogt, dsk, nw, head_lanes)


def _ssd_step_conv_kernel(xbc_ref, hist_ref, dt_ref, cw_ref, cb_ref, dtb_ref, alog_ref, xc_ref, dto_ref, da_ref):
    taps = cw_ref.shape[0]
    conv = cb_ref[...] + cw_ref[taps - 1:taps, :] * xbc_ref[...]
    for w in range(taps - 1):
        conv = conv + cw_ref[w:w + 1, :] * hist_ref[w]
    xc_ref[...] = _silu(conv)
    dt = _softplus(dt_ref[...] + dtb_ref[...])
    dto_ref[...] = dt
    da_ref[...] = jnp.exp(dt * -jnp.exp(alog_ref[...]))


def _ssd_step_kernel(xt_ref, zt_ref, bm_ref, cm_ref, dt_ref, da_ref, s0_ref, dsk_ref, nwt_ref, ot_ref, sn_ref,
                     *, n_heads, rep):
    xt = xt_ref[...]
    xdt = xt * dt_ref[...]
    lane = lax.broadcasted_iota(jnp.int32, xt.shape, 1)
    y = jnp.zeros(xt.shape, F32)
    for h in range(n_heads):
        g = h // rep
        s_new = da_ref[:, h:h + 1] * s0_ref[h] + xdt[:, h:h + 1] * bm_ref[g:g + 1, :]
        sn_ref[h] = s_new
        y = jnp.where(lane == h, jnp.sum(s_new * cm_ref[g:g + 1, :], axis=1, keepdims=True), y)
    yg = (y + dsk_ref[...] * xt) * _silu(zt_ref[...])
    ms = jnp.sum(jnp.sum(yg * yg, axis=1, keepdims=True), axis=0, keepdims=True) / yg.size
    ot_ref[...] = yg * lax.rsqrt(ms + RMS_EPS) * nwt_ref[...]


def _ssd_sample(xbc, z, dt_raw, conv_hist, s0, params, nw_t, lyr):
    conv_w, conv_b, dtb, _, alog, _, dsk, _, n_groups = params
    nb, n_heads, hp, ns = s0.shape
    ch, width = xbc.shape[1], z.shape[1]
    taps = conv_w.shape[1]
    full = lambda shape: pl.BlockSpec(shape, lambda i: (0,) * len(shape))
    lyr1 = lambda *shape: pl.BlockSpec((None,) + shape, lambda i: (lyr,) + (0,) * len(shape))
    xc, dt, da = pl.pallas_call(
        _ssd_step_conv_kernel,
        grid=(1,),
        in_specs=[full((nb, ch)), full((taps - 1, nb, ch)), full((nb, LANES)),
                  lyr1(taps, ch), lyr1(1, ch), lyr1(1, LANES), lyr1(1, LANES)],
        out_specs=[full((nb, ch)), full((nb, LANES)), full((nb, LANES))],
        out_shape=[jax.ShapeDtypeStruct((nb, ch), F32), jax.ShapeDtypeStruct((nb, LANES), F32),
                   jax.ShapeDtypeStruct((nb, LANES), F32)],
        compiler_params=_cparams("arbitrary"),
    )(xbc, jnp.transpose(conv_hist, (1, 0, 2)), dt_raw, conv_w, conv_b, dtb, alog)
    to_t = lambda v: jnp.transpose(v.reshape(nb, n_heads, hp), (0, 2, 1))
    gn = n_groups * ns
    seq = lambda *shape: pl.BlockSpec((None,) + shape, lambda b: (b,) + (0,) * len(shape))
    lyr_b = lambda *shape: pl.BlockSpec((None,) + shape, lambda b: (lyr,) + (0,) * len(shape))
    o_t, s_new = pl.pallas_call(
        functools.partial(_ssd_step_kernel, n_heads=n_heads, rep=n_heads // n_groups),
        grid=(nb,),
        in_specs=[seq(hp, n_heads), seq(hp, n_heads), seq(n_groups, ns), seq(n_groups, ns),
                  seq(1, n_heads), seq(1, n_heads), seq(n_heads, hp, ns), lyr_b(1, n_heads), lyr_b(hp, n_heads)],
        out_specs=[seq(hp, n_heads), seq(n_heads, hp, ns)],
        out_shape=[jax.ShapeDtypeStruct((nb, hp, n_heads), F32), jax.ShapeDtypeStruct((nb, n_heads, hp, ns), F32)],
        compiler_params=_cparams("arbitrary"),
    )(to_t(xc[:, :width]), to_t(z), xc[:, width:width + gn].reshape(nb, n_groups, ns),
      xc[:, width + gn:].reshape(nb, n_groups, ns), dt[:, None, :n_heads], da[:, None, :n_heads], s0,
      dsk[:, :, ::hp], nw_t)
    o = jnp.transpose(o_t, (0, 2, 1)).reshape(nb, width).astype(BF16)
    return o, s_new


def _pad_rows_per_seq(a, rows):
    n_seq, c = a.shape
    return jnp.pad(a[:, None, :], ((0, 0), (0, rows - 1), (0, 0))).reshape(n_seq * rows, c)


def _trunk(xp, xs, mods_p, mods_s, wts, dims, states):
    (w1, w3, w2, w_main, w_dt, w_gates, w_branch, w_out, w_glu, ln_g, ln_b, s5_params, s5_d,
     attn_sink, bias_p, bias_s, ssd_params, nw_t) = wts
    depth, batch, seq, dec_b = dims["depth"], dims["batch"], dims["seq"], dims["dec_b"]
    n_kv, hd, wn = dims["n_kv"], dims["hd"], dims["window"]
    s5_w, ssd_w, taps, ch = dims["s5_w"], dims["ssd_w"], dims["taps"], dims["conv_ch"]
    ssd_heads, hp, ns = dims["ssd_heads"], dims["ssd_hp"], dims["ssd_state"]
    alpha = (2 * depth) ** 0.25
    kw = n_kv * hd
    qw = attn_sink.shape[1] * hd
    o_q = s5_w
    o_z = o_q + qw + 2 * kw
    o_xbc = o_z + ssd_w
    sq = SAMPLE_Q_ROWS

    p_out = {k: [] for k in ("s5_re", "s5_im", "ssm", "conv", "k", "v")}
    s_out = {k: [] for k in p_out}
    x = (xp, xs)
    mods = (mods_p, mods_s)
    h = tuple(_modulate(x[i], mods[i], 0) for i in range(2))

    def norm(x, y, ls, next_ls, coef):
        res = [_deepnorm(x[i], y[i], mods[i], ls, ln_g, ln_b, next_ls, alpha, coef) for i in range(2)]
        return (res[0][0], res[1][0]), (res[0][1], res[1][1])

    for lyr in range(depth):
        ls = 3 * lyr
        y = _mm(*_ffn_up(*h, w1, w3, 2 * lyr), w2, 2 * lyr)
        x, h = norm(x, y, ls, ls + 1, 0.5)

        proj = lambda off, n: _mm(*h, w_main, lyr, col_off=off, n=n)
        u = proj(0, s5_w)
        qkv = proj(o_q, qw + 2 * kw)
        z = proj(o_z, ssd_w)
        xbc = proj(o_xbc, ch)
        dt_raw = _mm(*h, w_dt, lyr)
        gates = _mm(*h, w_gates, lyr, act="sigmoid")

        a_p, re_p, im_p = _s5_prompt(u[0], _s5_prompt_operators(s5_params(lyr), s5_d[lyr]), batch, seq)
        a_s, re_s, im_s = _s5_sample(u[1], _s5_sample_operators(s5_params(lyr), s5_d[lyr]),
                                     states["s5_re"][lyr], states["s5_im"][lyr])
        for out, re, im in ((p_out, re_p, im_p), (s_out, re_s, im_s)):
            out["s5_re"].append(re)
            out["s5_im"].append(im)
        o_a = (_s5_glu(a_p, w_glu, lyr), _s5_glu(a_s, w_glu, lyr))

        n_blk = seq // wn
        cur = lambda b, n: b * n_blk + n
        prev = lambda b, n: b * n_blk + jnp.maximum(n - 1, 0)
        k_col, v_col = qw // kw, qw // kw + 1
        qkv_p, qkv_s = qkv
        ob_p = _swa(qkv_p, (qkv_p, prev, k_col), (qkv_p, cur, k_col), (qkv_p, prev, v_col), (qkv_p, cur, v_col),
                    bias_p, attn_sink[lyr], batch, n_blk, wn, n_kv, hd, first_has_prev=False)
        kv3 = qkv_p.reshape(batch, seq, qw + 2 * kw)
        p_out["k"].append(kv3[:, seq - wn:, qw:qw + kw].reshape(batch, wn, n_kv, hd))
        p_out["v"].append(kv3[:, seq - wn:, qw + kw:].reshape(batch, wn, n_kv, hd))
        one = lambda b, n: b
        k_new, v_new = qkv_s[:, qw:qw + kw], qkv_s[:, qw + kw:]
        k_cache = states["k"][lyr].reshape(dec_b * wn, kw)
        v_cache = states["v"][lyr].reshape(dec_b * wn, kw)
        ob_s = _swa(_pad_rows_per_seq(qkv_s[:, :qw], sq), (k_cache, one, 0), (_pad_rows_per_seq(k_new, wn), one, 0),
                    (v_cache, one, 0), (_pad_rows_per_seq(v_new, wn), one, 0),
                    bias_s, attn_sink[lyr], dec_b, 1, wn, n_kv, hd, first_has_prev=True)
        ob_s = ob_s.reshape(dec_b, sq, qw)[:, 0]
        s_out["k"].append(jnp.concatenate([states["k"][lyr][:, 1:], k_new.reshape(dec_b, 1, n_kv, hd)], axis=1))
        s_out["v"].append(jnp.concatenate([states["v"][lyr][:, 1:], v_new.reshape(dec_b, 1, n_kv, hd)], axis=1))

        cq = SSD_CHUNK
        oc_p, ssm_p = _ssd(xbc[0], z[0], dt_raw[0], jnp.zeros((batch, SUBLANES, ch), F32),
                           jnp.zeros((batch, ssd_heads, hp, ns), F32), ssd_params, lyr, batch, seq // cq)
        p_out["conv"].append(xbc[0].reshape(batch, seq, ch)[:, seq - (taps - 1):])
        conv_hist = states["conv"][lyr]
        oc_s, ssm_s = _ssd_sample(xbc[1], z[1], dt_raw[1], conv_hist, states["ssm"][lyr], ssd_params, nw_t, lyr)
        s_out["conv"].append(jnp.concatenate([conv_hist[:, 1:], xbc[1][:, None, :]], axis=1))
        p_out["ssm"].append(ssm_p)
        s_out["ssm"].append(ssm_s)

        merged = _merge((o_a[0], ob_p, oc_p), (o_a[1], ob_s, oc_s), w_branch, lyr, *gates)
        y = _mm(*merged, w_out, lyr, tn_cap=256)
        x, h = norm(x, y, ls + 1, ls + 2, 1.0)

        y = _mm(*_ffn_up(*h, w1, w3, 2 * lyr + 1), w2, 2 * lyr + 1)
        x, h = norm(x, y, ls + 2, ls + 3 if lyr + 1 < depth else None, 0.5)

    names = ("s5_re", "s5_im", "ssm", "conv", "k", "v")
    return x, tuple(jnp.stack(p_out[k]) for k in names), tuple(jnp.stack(s_out[k]) for k in names)


def kernel(x_prompt, x_sample, state_s5_re, state_s5_im, state_ssm, state_conv, cache_k_win, cache_v_win,
           c_prompt, c_sample, w_ada, b_ada, ln_g, ln_b, ffn_w1, ffn_w3, ffn_w2, w_in, w_branch, w_out,
           s5_lam_re, s5_lam_im, s5_log_dt, s5_b_re, s5_b_im, s5_c_re, s5_c_im, s5_d, s5_w_glu,
           attn_sink, rel_bias, conv_w, conv_b, dt_bias, a_log, ssd_d, ssd_norm_w):
    batch, seq, d = x_prompt.shape
    dec_b, dec_seq, _ = x_sample.shape
    assert dec_seq == 1, "the sample group decodes one token per sequence"
    depth, n_sub = w_ada.shape[:2]
    d_ff = ffn_w1.shape[-1]
    g5, p5, i5 = s5_b_re.shape[1:]
    s5_w = g5 * i5
    n_heads = attn_sink.shape[1]
    _, _, wn, n_kv, hd = cache_k_win.shape
    qw, kw = n_heads * hd, n_kv * hd
    _, _, ssd_heads, hp, ns = state_ssm.shape
    ssd_w = ssd_heads * hp
    taps, ch = conv_w.shape[1:]
    n_groups = (ch - ssd_w) // (2 * ns)
    assert seq % wn == 0 and seq % SSD_CHUNK == 0 and seq % S5_CHUNK == 0
    assert ssd_heads <= LANES and taps - 1 <= SUBLANES and LANES % i5 == 0 and s5_w % LANES == 0

    bf = lambda w: w.astype(BF16)
    w1 = ffn_w1.reshape(depth * 2, d, d_ff)
    w3 = ffn_w3.reshape(depth * 2, d, d_ff)
    w2 = bf(ffn_w2).reshape(depth * 2, d_ff, d)
    o_dt = s5_w + qw + 2 * kw + ssd_w + ch
    o_g = o_dt + ssd_heads
    w_main = bf(w_in[:, :, :o_dt])
    w_dt = jnp.pad(bf(w_in[:, :, o_dt:o_g]), ((0, 0), (0, 0), (0, LANES - ssd_heads)))
    w_gates = bf(w_in[:, :, o_g:])

    lane_pad = lambda v: jnp.pad(v, ((0, 0), (0, LANES - v.shape[1])))[:, None, :]
    ssd_params = (conv_w, conv_b[:, None, :], lane_pad(dt_bias), dt_bias[:, :, None],
                  lane_pad(a_log), a_log[:, :, None], jnp.repeat(ssd_d, hp, axis=1)[:, None, :],
                  ssd_norm_w[:, None, :], n_groups)
    s5_params = lambda lyr: (s5_lam_re[lyr], s5_lam_im[lyr], s5_log_dt[lyr], s5_b_re[lyr], s5_b_im[lyr],
                             s5_c_re[lyr], s5_c_im[lyr])
    bias_p = _rel_bias_table(rel_bias, wn, n_kv)
    rep = n_heads // n_kv
    bias_s = bias_p.reshape(n_kv, rep, wn, 2 * wn)[:, :, :SAMPLE_Q_ROWS].reshape(n_kv, rep * SAMPLE_Q_ROWS, 2 * wn)
    wts = (w1, w3, w2, w_main, w_dt, w_gates, bf(w_branch), w_out, bf(s5_w_glu),
           ln_g.reshape(depth * n_sub, 1, d), ln_b.reshape(depth * n_sub, 1, d), s5_params, s5_d,
           attn_sink, bias_p, bias_s, ssd_params,
           jnp.transpose(ssd_norm_w.reshape(depth, ssd_heads, hp), (0, 2, 1)))

    rows = -(-(dec_b + batch) // SUBLANES) * SUBLANES
    c_all = jnp.pad(jnp.concatenate([c_sample, c_prompt], axis=0), ((0, rows - dec_b - batch), (0, 0)))
    mods = _adaln(c_all, w_ada.reshape(depth * n_sub, d, 3 * d), b_ada.reshape(depth * n_sub, 1, 3 * d))

    dims = dict(depth=depth, batch=batch, seq=seq, dec_b=dec_b, n_kv=n_kv, hd=hd, window=wn, s5_w=s5_w,
                ssd_w=ssd_w, ssd_heads=ssd_heads, ssd_hp=hp, ssd_state=ns, taps=taps, conv_ch=ch)
    states = dict(s5_re=state_s5_re, s5_im=state_s5_im, ssm=state_ssm, conv=state_conv, k=cache_k_win, v=cache_v_win)
    (yp, ys), p_state, s_state = _trunk(x_prompt.reshape(batch * seq, d), x_sample.reshape(dec_b, d),
                                        _Mods(mods, True, dec_b, seq, d), _Mods(mods, False, dec_b, 1, d),
                                        wts, dims, states)
    return (yp.reshape(batch, seq, d), ys.reshape(dec_b, 1, d)) + p_state + s_state
```

```python
import functools
import math

import jax
import jax.numpy as jnp
from jax import lax
from jax.experimental import pallas as pl
from jax.experimental.pallas import tpu as pltpu

F32 = jnp.float32
BF16 = jnp.bfloat16

LN_EPS = 1e-5
RMS_EPS = 1e-5
SSD_CHUNK = 128
S5_CHUNK = 16
LANES = 128
SUBLANES = 8
SAMPLE_Q_ROWS = 16
VMEM_BYTES_V7X = 64 * 1024 * 1024
VMEM_LIMIT = VMEM_BYTES_V7X - 8 * 1024 * 1024
NEG_BIG = -1e30

_NT = (((1,), (1,)), ((), ()))
_TN = (((0,), (0,)), ((), ()))


def _pick_tile(n, cap, align):
    if n <= cap:
        return n
    t = (cap // align) * align
    while t >= align:
        if n % t == 0:
            return t
        t -= align
    raise ValueError(f"no tile for {n} under {cap} aligned to {align}")


def _cparams(*sem):
    return pltpu.CompilerParams(dimension_semantics=sem, vmem_limit_bytes=VMEM_LIMIT)


def _sigmoid(x):
    return 1.0 / (1.0 + jnp.exp(-x))


def _silu(x):
    return x * _sigmoid(x)


def _softplus(x):
    return jnp.maximum(x, 0.0) + jnp.log1p(jnp.exp(-jnp.abs(x)))


def _dot(a, b, precision=None):
    return jnp.dot(a, b, preferred_element_type=F32, precision=precision)


def _adaln_kernel(c_ref, w_ref, b_ref, o_ref):
    a = _silu(c_ref[...]).astype(BF16)
    o_ref[...] = _dot(a, w_ref[...].astype(BF16)) + b_ref[...]


def _adaln(c_all, w_ada, b_ada):
    n_ls, d, n3 = w_ada.shape
    rows = c_all.shape[0]
    tn = _pick_tile(n3, 512, LANES)
    return pl.pallas_call(
        _adaln_kernel,
        grid=(n_ls, n3 // tn),
        in_specs=[pl.BlockSpec((rows, d), lambda s, j: (0, 0)),
                  pl.BlockSpec((None, d, tn), lambda s, j: (s, 0, j)),
                  pl.BlockSpec((None, 1, tn), lambda s, j: (s, 0, j))],
        out_specs=pl.BlockSpec((None, rows, tn), lambda s, j: (s, 0, j)),
        out_shape=jax.ShapeDtypeStruct((n_ls, rows, n3), F32),
        compiler_params=_cparams("arbitrary", "arbitrary"),
    )(c_all, w_ada, b_ada)


class _Mods:
    def __init__(self, mods, prompt, dec_b, seq, d):
        self.prompt, self.dec_b, self.seq, self.d = prompt, dec_b, seq, d
        n_ls, rows, n3 = mods.shape
        self.arr = mods.reshape(n_ls, rows, 1, n3) if prompt else mods

    def spec(self, ls, part, tm):
        d = self.d
        if self.prompt:
            dec_b, seq = self.dec_b, self.seq
            return pl.BlockSpec((None, None, 1, d), lambda i, *_: (ls, dec_b + (i * tm) // seq, 0, part))
        return pl.BlockSpec((None, self.dec_b, d), lambda i, *_: (ls, 0, part))


def _modulate_kernel(x_ref, sh_ref, sc_ref, h_ref):
    h_ref[...] = (x_ref[...] * (1.0 + sc_ref[...]) + sh_ref[...]).astype(BF16)


def _modulate(x, mods, ls):
    m, d = x.shape
    tm = _pick_tile(mods.seq, 256, 16) if mods.prompt else m
    return pl.pallas_call(
        _modulate_kernel,
        grid=(m // tm,),
        in_specs=[pl.BlockSpec((tm, d), lambda i: (i, 0)), mods.spec(ls, 0, tm), mods.spec(ls, 1, tm)],
        out_specs=pl.BlockSpec((tm, d), lambda i: (i, 0)),
        out_shape=jax.ShapeDtypeStruct((m, d), BF16),
        compiler_params=_cparams("arbitrary"),
    )(x, mods.arr, mods.arr)


def _deepnorm_kernel(x_ref, y_ref, gate_ref, g_ref, b_ref, *rest, alpha, coef, with_h):
    t = alpha * x_ref[...] + (1.0 + gate_ref[...]) * (coef * y_ref[...])
    mu = jnp.mean(t, axis=-1, keepdims=True)
    dlt = t - mu
    var = jnp.mean(dlt * dlt, axis=-1, keepdims=True)
    xn = dlt * lax.rsqrt(var + LN_EPS) * g_ref[...] + b_ref[...]
    if with_h:
        sh_ref, sc_ref, xo_ref, ho_ref = rest
        ho_ref[...] = (xn * (1.0 + sc_ref[...]) + sh_ref[...]).astype(BF16)
    else:
        (xo_ref,) = rest
    xo_ref[...] = xn


def _deepnorm(x, y, mods, ls, ln_g, ln_b, next_ls, alpha, coef):
    m, d = x.shape
    tm = _pick_tile(mods.seq, 256, 16) if mods.prompt else m
    with_h = next_ls is not None
    row = pl.BlockSpec((tm, d), lambda i: (i, 0))
    vec = pl.BlockSpec((None, 1, d), lambda i: (ls, 0, 0))
    in_specs = [row, row, mods.spec(ls, 2, tm), vec, vec]
    args = [x, y, mods.arr, ln_g, ln_b]
    out_specs, out_shape = [row], [jax.ShapeDtypeStruct((m, d), F32)]
    if with_h:
        in_specs += [mods.spec(next_ls, 0, tm), mods.spec(next_ls, 1, tm)]
        args += [mods.arr, mods.arr]
        out_specs.append(row)
        out_shape.append(jax.ShapeDtypeStruct((m, d), BF16))
    res = pl.pallas_call(
        functools.partial(_deepnorm_kernel, alpha=alpha, coef=coef, with_h=with_h),
        grid=(m // tm,), in_specs=in_specs, out_specs=out_specs, out_shape=out_shape,
        compiler_params=_cparams("arbitrary"),
    )(*args)
    return (res[0], res[1]) if with_h else (res[0], None)


def _col_slab(j, tn):
    return pl.ds(pl.multiple_of(j * tn, tn), tn)


def _mm_kernel(x_ref, xs_ref, w_ref, o_ref, os_ref, *, act, tn):
    def apply(x):
        r = _dot(x, w_ref[...].astype(BF16))
        return _sigmoid(r) if act == "sigmoid" else r

    o_ref[...] = apply(x_ref[...]).astype(o_ref.dtype)

    @pl.when(pl.program_id(0) == 0)
    def _():
        os_ref[:, _col_slab(pl.program_id(1), tn)] = apply(xs_ref[...]).astype(os_ref.dtype)


def _row_tile(m, k):
    cap = 2048 if k <= 4096 else 512
    return _pick_tile(m, cap, 16)


def _row_spec(tm, k, single):
    mode = dict(pipeline_mode=pl.Buffered(1)) if single else {}
    return pl.BlockSpec((tm, k), lambda i, j: (i, 0), **mode)


def _full_spec(shape):
    return pl.BlockSpec(shape, lambda i, j: (0,) * len(shape))


def _mm(x, xs, w, widx, col_off=0, n=None, tn_cap=512, act=None, out_dtype=F32):
    m, k = x.shape
    n = w.shape[-1] if n is None else n
    tm, tn = _row_tile(m, k), _pick_tile(math.gcd(n, col_off) if col_off else n, tn_cap, LANES)
    col0 = col_off // tn
    single = tm * k * 2 > 12 * 1024 * 1024
    ms = xs.shape[0]
    return pl.pallas_call(
        functools.partial(_mm_kernel, act=act, tn=tn),
        grid=(m // tm, n // tn),
        in_specs=[_row_spec(tm, k, single), _full_spec((ms, k)),
                  pl.BlockSpec((None, k, tn), lambda i, j: (widx, 0, col0 + j))],
        out_specs=[pl.BlockSpec((tm, tn), lambda i, j: (i, j)), _full_spec((ms, n))],
        out_shape=[jax.ShapeDtypeStruct((m, n), out_dtype), jax.ShapeDtypeStruct((ms, n), out_dtype)],
        compiler_params=_cparams("arbitrary", "arbitrary"),
    )(x, xs, w)


def _ffn_up_kernel(h_ref, hs_ref, w1_ref, w3_ref, o_ref, os_ref, *, tn):
    def apply(h):
        a = _dot(h, w1_ref[...].astype(BF16))
        return (_silu(a) * _dot(h, w3_ref[...].astype(BF16))).astype(BF16)

    o_ref[...] = apply(h_ref[...])

    @pl.when(pl.program_id(0) == 0)
    def _():
        os_ref[:, _col_slab(pl.program_id(1), tn)] = apply(hs_ref[...])


def _ffn_up(h, hs, w1, w3, widx):
    m, k = h.shape
    n = w1.shape[-1]
    tm, tn = _row_tile(m, k), _pick_tile(n, 256, LANES)
    ms = hs.shape[0]
    wspec = pl.BlockSpec((None, k, tn), lambda i, j: (widx, 0, j))
    return pl.pallas_call(
        functools.partial(_ffn_up_kernel, tn=tn),
        grid=(m // tm, n // tn),
        in_specs=[_row_spec(tm, k, tm * k * 2 > 12 * 1024 * 1024), _full_spec((ms, k)), wspec, wspec],
        out_specs=[pl.BlockSpec((tm, tn), lambda i, j: (i, j)), _full_spec((ms, n))],
        out_shape=[jax.ShapeDtypeStruct((m, n), BF16), jax.ShapeDtypeStruct((ms, n), BF16)],
        compiler_params=_cparams("arbitrary", "arbitrary"),
    )(h, hs, w1, w3)


def _merge_kernel(oa_ref, ob_ref, oc_ref, sa_ref, sb_ref, sc_ref, wa_ref, wb_ref, wc_ref,
                  ga_ref, gb_ref, gc_ref, gs_ref, o_ref, os_ref, *, tn, d):
    def apply(ga, gb, gc, a, b, c):
        return (ga * _dot(a, wa_ref[...]) + gb * _dot(b, wb_ref[...]) + gc * _dot(c, wc_ref[...])).astype(BF16)

    o_ref[...] = apply(ga_ref[...], gb_ref[...], gc_ref[...], oa_ref[...], ob_ref[...], oc_ref[...])

    @pl.when(pl.program_id(0) == 0)
    def _():
        j = pl.program_id(1)
        gate = lambda part: gs_ref[:, pl.ds(pl.multiple_of(part * d + j * tn, tn), tn)]
        os_ref[:, _col_slab(j, tn)] = apply(gate(0), gate(1), gate(2), sa_ref[...], sb_ref[...], sc_ref[...])


def _merge(branches_p, branches_s, w_branch, lyr, gates_p, gates_s):
    o_a, o_b, o_c = branches_p
    m, ka = o_a.shape
    kb, kc = o_b.shape[1], o_c.shape[1]
    d = w_branch.shape[-1]
    ms = gates_s.shape[0]
    assert ka == kb and kc % (ka + kb) == 0, "branch widths must tile the stacked branch weight"
    tm, tn = _pick_tile(m, 1024, 16), _pick_tile(d, 512, LANES)
    nj = d // tn
    rows = lambda width: pl.BlockSpec((tm, width), lambda i, j: (i, 0))
    return pl.pallas_call(
        functools.partial(_merge_kernel, tn=tn, d=d),
        grid=(m // tm, nj),
        in_specs=[rows(ka), rows(kb), rows(kc),
                  _full_spec((ms, ka)), _full_spec((ms, kb)), _full_spec((ms, kc)),
                  pl.BlockSpec((None, ka, tn), lambda i, j: (lyr, 0, j)),
                  pl.BlockSpec((None, kb, tn), lambda i, j: (lyr, 1, j)),
                  pl.BlockSpec((None, kc, tn), lambda i, j: (lyr, (ka + kb) // kc, j)),
                  pl.BlockSpec((tm, tn), lambda i, j: (i, j)),
                  pl.BlockSpec((tm, tn), lambda i, j: (i, nj + j)),
                  pl.BlockSpec((tm, tn), lambda i, j: (i, 2 * nj + j)),
                  _full_spec((ms, 3 * d))],
        out_specs=[pl.BlockSpec((tm, tn), lambda i, j: (i, j)), _full_spec((ms, d))],
        out_shape=[jax.ShapeDtypeStruct((m, d), BF16), jax.ShapeDtypeStruct((ms, d), BF16)],
        compiler_params=_cparams("arbitrary", "arbitrary"),
    )(o_a, o_b, o_c, *branches_s, w_branch, w_branch, w_branch, gates_p, gates_p, gates_p, gates_s)


def _cmul(ar, ai, br, bi):
    return ar * br - ai * bi, ar * bi + ai * br


def _gelu_tanh(x):
    return jax.nn.gelu(x, approximate=True)


def _s5_tile_kernel(u_ref, kd_ref, bd_ref, cd_ref, apr_ref, api_ref, d_ref, a_ref, hr_ref, hi_ref,
                    w_scr, shi_scr, slo_scr, e_scr, hp_scr, y_scr, *, q, n_chunks, nb):
    lt = LANES
    sw = bd_ref.shape[-1] // 2
    zero = jnp.zeros((lt, lt), BF16)
    for s in range(q):
        for l in range(q):
            w_scr[s * lt:(s + 1) * lt, l * lt:(l + 1) * lt] = kd_ref[l - s] if l >= s else zero
    b_re, b_im = bd_ref[:, :sw], bd_ref[:, sw:]
    for s in range(q):
        s_re, s_im = _cmul(b_re, b_im, apr_ref[q - 1 - s:q - s, :], api_ref[q - 1 - s:q - s, :])
        s_full = jnp.concatenate([s_re, s_im], axis=1)
        s_hi = s_full.astype(BF16)
        shi_scr[s * lt:(s + 1) * lt, :] = s_hi
        slo_scr[s * lt:(s + 1) * lt, :] = (s_full - s_hi.astype(F32)).astype(BF16)
    u = u_ref[...]
    u_hi = u.astype(BF16)
    u_lo = (u - u_hi.astype(F32)).astype(BF16)
    e_scr[...] = _dot(u_hi, shi_scr[...]) + _dot(u_lo, shi_scr[...]) + _dot(u_hi, slo_scr[...])
    aq_re, aq_im = apr_ref[q:q + 1, :], api_ref[q:q + 1, :]
    h_re = jnp.zeros((nb, sw), F32)
    h_im = jnp.zeros((nb, sw), F32)
    for c in range(n_chunks):
        rows = slice(c * nb, (c + 1) * nb)
        hp_scr[rows, :sw] = h_re
        hp_scr[rows, sw:] = h_im
        t_re, t_im = _cmul(h_re, h_im, aq_re, aq_im)
        h_re, h_im = t_re + e_scr[rows, :sw], t_im + e_scr[rows, sw:]
    hr_ref[...] = h_re
    hi_ref[...] = h_im
    y_scr[...] = _dot(u_hi, w_scr[...])
    p_re, p_im = hp_scr[:, :sw], hp_scr[:, sw:]
    cd = cd_ref[...]
    for l in range(q):
        t_re, t_im = _cmul(p_re, p_im, apr_ref[l + 1:l + 2, :], api_ref[l + 1:l + 2, :])
        adv = jnp.concatenate([t_re, t_im], axis=1).astype(BF16)
        y_scr[:, l * lt:(l + 1) * lt] += _dot(adv, cd)
    a_ref[...] = _gelu_tanh(y_scr[...] + d_ref[...] * u).astype(BF16)


def _s5_prompt(u, ops, n_seq, seq):
    kd, bd, cd, ap_re, ap_im, d_t, gpt, p = ops
    q = S5_CHUNK
    n_t, n_c, width = kd.shape[1], seq // q, u.shape[1]
    rows, cols, sw = n_c * n_seq, q * LANES, bd.shape[-1] // 2
    u_t = jnp.transpose(u.reshape(n_seq, n_c, q, n_t, LANES), (3, 1, 0, 2, 4)).reshape(n_t, rows, cols)
    blk = lambda *shape: pl.BlockSpec((None,) + shape, lambda i: (i,) + (0,) * len(shape))
    a_t, h_re, h_im = pl.pallas_call(
        functools.partial(_s5_tile_kernel, q=q, n_chunks=n_c, nb=n_seq),
        grid=(n_t,),
        in_specs=[blk(rows, cols), pl.BlockSpec((q, None, LANES, LANES), lambda i: (0, i, 0, 0)),
                  blk(LANES, 2 * sw), blk(2 * sw, LANES), blk(q + 1, sw), blk(q + 1, sw), blk(1, cols)],
        out_specs=[blk(rows, cols), blk(n_seq, sw), blk(n_seq, sw)],
        out_shape=[jax.ShapeDtypeStruct((n_t, rows, cols), BF16),
                   jax.ShapeDtypeStruct((n_t, n_seq, sw), F32), jax.ShapeDtypeStruct((n_t, n_seq, sw), F32)],
        scratch_shapes=[pltpu.VMEM((cols, cols), BF16), pltpu.VMEM((cols, 2 * sw), BF16),
                        pltpu.VMEM((cols, 2 * sw), BF16), pltpu.VMEM((rows, 2 * sw), F32),
                        pltpu.VMEM((rows, 2 * sw), F32), pltpu.VMEM((rows, cols), F32)],
        compiler_params=_cparams("arbitrary"),
    )(u_t, kd, bd, cd, ap_re, ap_im, d_t)
    a = jnp.transpose(a_t.reshape(n_t, n_c, n_seq, q, LANES), (2, 1, 3, 0, 4)).reshape(n_seq * seq, width)
    to_state = lambda h: jnp.transpose(h.reshape(n_t, n_seq, gpt, p), (1, 0, 2, 3)).reshape(n_seq, n_t * gpt, p)
    return a, to_state(h_re), to_state(h_im)


def _s5_step_kernel(u_ref, st_ref, tt_ref, rt_ref, a1_ref, a2_ref, d_ref, h0_ref, a_ref, hf_ref):
    hi = lax.Precision.HIGHEST
    u, h0 = u_ref[...], h0_ref[...]
    half = h0.shape[-1] // 2
    hf_ref[...] = h0 * a1_ref[...] + pltpu.roll(h0, half, axis=1) * a2_ref[...] + _dot(u, st_ref[...], hi)
    y = _dot(u, tt_ref[...], hi) + _dot(h0, rt_ref[...], hi)
    a_ref[...] = _gelu_tanh(y + d_ref[...] * u).astype(BF16)


def _s5_sample(u, ops, s5_re, s5_im):
    st, tt, rt, a1, a2, d_g = ops
    g, kq, p2 = st.shape
    nb, i_ch = u.shape[0], u.shape[1] // g
    u_t = jnp.pad(jnp.transpose(u.reshape(nb, g, i_ch), (1, 0, 2)), ((0, 0), (0, 0), (0, kq - i_ch)))
    h0 = jnp.concatenate([jnp.transpose(s5_re, (1, 0, 2)), jnp.transpose(s5_im, (1, 0, 2))], axis=-1)
    blk = lambda *shape: pl.BlockSpec((None,) + shape, lambda i: (i,) + (0,) * len(shape))
    a_t, hf = pl.pallas_call(
        _s5_step_kernel,
        grid=(g,),
        in_specs=[blk(nb, kq), blk(kq, p2), blk(kq, kq), blk(p2, kq), blk(1, p2), blk(1, p2), blk(1, kq), blk(nb, p2)],
        out_specs=[blk(nb, kq), blk(nb, p2)],
        out_shape=[jax.ShapeDtypeStruct((g, nb, kq), BF16), jax.ShapeDtypeStruct((g, nb, p2), F32)],
        compiler_params=_cparams("arbitrary"),
    )(u_t, st, tt, rt, a1, a2, d_g, h0)
    a = jnp.transpose(a_t[:, :, :i_ch], (1, 0, 2)).reshape(nb, g * i_ch)
    half = p2 // 2
    return a, jnp.transpose(hf[:, :, :half], (1, 0, 2)), jnp.transpose(hf[:, :, half:], (1, 0, 2))


def _s5_basis(lam_re, lam_im, log_dt, b_re, b_im, c_re, c_im, q):
    hi = lax.Precision.HIGHEST
    dt = jnp.exp(log_dt)[:, None]
    t = jnp.arange(q + 1, dtype=F32)[:, None, None]
    mag = jnp.exp(lam_re[None] * dt[None] * t)
    ang = lam_im[None] * dt[None] * t
    ap_re, ap_im = mag * jnp.cos(ang), mag * jnp.sin(ang)
    nr, ni = ap_re[1] - 1.0, ap_im[1]
    den = lam_re * lam_re + lam_im * lam_im
    f_re, f_im = (nr * lam_re + ni * lam_im) / den, (ni * lam_re - nr * lam_im) / den
    bb_re, bb_im = _cmul(f_re[..., None], f_im[..., None], b_re, b_im)
    ab_re, ab_im = _cmul(ap_re[:q, :, :, None], ap_im[:q, :, :, None], bb_re[None], bb_im[None])
    k_t = (jnp.einsum("gip,tgpj->tgij", c_re, ab_re, precision=hi)
           - jnp.einsum("gip,tgpj->tgij", c_im, ab_im, precision=hi))
    return ap_re, ap_im, bb_re, bb_im, k_t


def _s5_prompt_operators(params, d_skip):
    lam_re, lam_im, log_dt, b_re, b_im, c_re, c_im = params
    q = S5_CHUNK
    g, p, i_ch = b_re.shape
    gpt = LANES // i_ch
    n_t = g // gpt
    ap_re, ap_im, bb_re, bb_im, k_t = _s5_basis(*params, q)
    eye = jnp.eye(gpt, dtype=F32)
    kd = jnp.einsum("tqgij,gh->tqgjhi", k_t.reshape(q, n_t, gpt, i_ch, i_ch), eye).reshape(q, n_t, LANES, LANES)
    to_bd = lambda v: jnp.einsum("qgpj,gh->qgjhp", v.reshape(n_t, gpt, p, i_ch), eye).reshape(n_t, LANES, gpt * p)
    bd = jnp.concatenate([to_bd(bb_re), to_bd(bb_im)], axis=-1)
    to_cd = lambda v: jnp.einsum("qgip,gh->qgphi", v.reshape(n_t, gpt, i_ch, p), eye).reshape(n_t, gpt * p, LANES)
    cd = jnp.concatenate([to_cd(c_re), -to_cd(c_im)], axis=1)
    to_t = lambda v: jnp.transpose(v.reshape(q + 1, n_t, gpt * p), (1, 0, 2))
    d_t = jnp.tile(d_skip.reshape(n_t, 1, LANES), (1, 1, q))
    return kd.astype(BF16), bd, cd.astype(BF16), to_t(ap_re), to_t(ap_im), d_t, gpt, p


def _s5_sample_operators(params, d_skip):
    ap_re, ap_im, bb_re, bb_im, k_t = _s5_basis(*params, 1)
    c_re, c_im = params[5], params[6]
    g, p, i_ch = bb_re.shape
    pad = LANES - i_ch
    st = jnp.pad(jnp.concatenate([jnp.transpose(bb_re, (0, 2, 1)), jnp.transpose(bb_im, (0, 2, 1))], axis=-1),
                 ((0, 0), (0, pad), (0, 0)))
    tt = jnp.pad(jnp.transpose(k_t[0], (0, 2, 1)), ((0, 0), (0, pad), (0, pad)))
    rc_re, rc_im = _cmul(c_re, c_im, ap_re[1][:, None, :], ap_im[1][:, None, :])
    rt = jnp.pad(jnp.concatenate([jnp.transpose(rc_re, (0, 2, 1)), -jnp.transpose(rc_im, (0, 2, 1))], axis=1),
                 ((0, 0), (0, 0), (0, pad)))
    a1 = jnp.concatenate([ap_re[1], ap_re[1]], axis=-1)[:, None, :]
    a2 = jnp.concatenate([-ap_im[1], ap_im[1]], axis=-1)[:, None, :]
    d_g = jnp.pad(d_skip.reshape(g, 1, i_ch), ((0, 0), (0, 0), (0, pad)))
    return st, tt, rt, a1, a2, d_g


def _s5_glu_kernel(a_ref, w_ref, o_ref):
    r = _dot(a_ref[...], w_ref[...])
    wd = o_ref.shape[-1]
    o_ref[...] = (r[:, :wd] * _sigmoid(r[:, wd:])).astype(BF16)


def _s5_glu(a, w_glu, lyr):
    m, wd = a.shape
    tm = _pick_tile(m, 1024, 16)
    row = pl.BlockSpec((tm, wd), lambda i: (i, 0))
    return pl.pallas_call(
        _s5_glu_kernel,
        grid=(m // tm,),
        in_specs=[row, pl.BlockSpec((None, wd, 2 * wd), lambda i: (lyr, 0, 0))],
        out_specs=row,
        out_shape=jax.ShapeDtypeStruct((m, wd), BF16),
        compiler_params=_cparams("arbitrary"),
    )(a, w_glu)


def _swa_kernel(sink_ref, q_ref, kp_ref, kc_ref, vp_ref, vc_ref, bias_ref, o_ref,
                *, n_kv, rep, hd, first_has_prev, scale):
    n = pl.program_id(1)
    wq, wn = q_ref.shape[0], kc_ref.shape[0]
    q = q_ref[...]
    k = jnp.concatenate([kp_ref[...], kc_ref[...]], axis=0).astype(BF16)
    v = jnp.concatenate([vp_ref[...], vc_ref[...]], axis=0).astype(BF16)
    l_i = lax.broadcasted_iota(jnp.int32, (rep, wq, 2 * wn), 1).reshape(rep * wq, 2 * wn)
    k_j = lax.broadcasted_iota(jnp.int32, (rep * wq, 2 * wn), 1)
    dist = l_i + wn - k_j
    valid = (dist >= 0) & (dist < wn)
    if not first_has_prev:
        valid = valid & ((n > 0) | (k_j >= wn))
    for h in range(n_kv):
        heads = [h * rep + r for r in range(rep)]
        qs = jnp.concatenate([q[:, a * hd:(a + 1) * hd] for a in heads], axis=0).astype(BF16)
        kh, vh = k[:, h * hd:(h + 1) * hd], v[:, h * hd:(h + 1) * hd]
        s = lax.dot_general(qs, kh, _NT, preferred_element_type=F32) * scale + bias_ref[h]
        s = jnp.where(valid, s, NEG_BIG)
        sink = jnp.concatenate([jnp.full((wq, 1), sink_ref[a], F32) for a in heads], axis=0)
        mx = jnp.maximum(jnp.max(s, axis=1, keepdims=True), sink)
        p = jnp.exp(s - mx)
        den = jnp.sum(p, axis=1, keepdims=True) + jnp.exp(sink - mx)
        o = _dot(p.astype(BF16), vh) / den
        for r, a in enumerate(heads):
            o_ref[:, a * hd:(a + 1) * hd] = o[r * wq:(r + 1) * wq].astype(BF16)


def _swa(q_arr, kp, kc, vp, vc, bias, sink, n_seq, n_blk, wn, n_kv, hd, first_has_prev):
    n_heads = sink.shape[0]
    rep = n_heads // n_kv
    wq = bias.shape[1] // rep
    qw, kw = n_heads * hd, n_kv * hd

    def kv_spec(src):
        _, rmap, col = src
        return pl.BlockSpec((wn, kw), lambda b, n: (rmap(b, n), col))

    return pl.pallas_call(
        functools.partial(_swa_kernel, n_kv=n_kv, rep=rep, hd=hd, first_has_prev=first_has_prev,
                          scale=hd ** -0.5),
        grid=(n_seq, n_blk),
        in_specs=[pl.BlockSpec(memory_space=pltpu.SMEM),
                  pl.BlockSpec((wq, qw), lambda b, n: (b * n_blk + n, 0)),
                  kv_spec(kp), kv_spec(kc), kv_spec(vp), kv_spec(vc),
                  pl.BlockSpec(bias.shape, lambda b, n: (0, 0, 0))],
        out_specs=pl.BlockSpec((wq, qw), lambda b, n: (b * n_blk + n, 0)),
        out_shape=jax.ShapeDtypeStruct((n_seq * n_blk * wq, qw), BF16),
        compiler_params=_cparams("arbitrary", "arbitrary"),
    )(sink, q_arr, kp[0], kc[0], vp[0], vc[0], bias)


def _rel_bias_table(rel_bias, wn, n_kv):
    n_buckets, n_heads = rel_bias.shape
    max_exact = n_buckets // 2
    dist = (jnp.arange(wn)[:, None] + wn) - jnp.arange(2 * wn)[None, :]
    dist = jnp.clip(dist, 0, wn - 1)
    d = jnp.maximum(dist, 1).astype(F32)
    large = max_exact + (jnp.log(d / max_exact) / math.log(wn / max_exact)
                         * (n_buckets - max_exact)).astype(jnp.int32)
    bucket = jnp.where(dist < max_exact, dist, jnp.minimum(large, n_buckets - 1))
    bias = jnp.transpose(rel_bias[bucket], (2, 0, 1)).astype(F32)
    return bias.reshape(n_kv, (n_heads // n_kv) * wn, 2 * wn)


def _ssd_kernel(xbc_ref, z_ref, dt_ref, dtt_ref, conv0_ref, h0_ref, cw_ref, cb_ref, dtb_ref, dtbt_ref,
                alog_ref, alogt_ref, dsk_ref, nw_ref, ex_ref, o_ref, hf_ref, xp_scr, st_scr, y_scr, yoff_scr,
                *, n_heads, hp, n_groups, ns):
    hi = lax.Precision.HIGHEST
    c = pl.program_id(1)
    q = xbc_ref.shape[0]
    width = n_heads * hp
    rep = n_heads // n_groups
    taps = cw_ref.shape[0]

    @pl.when(c == 0)
    def _():
        xp_scr[0:SUBLANES, :] = conv0_ref[...]
        st_scr[...] = h0_ref[...]

    xp_scr[SUBLANES:SUBLANES + q, :] = xbc_ref[...]
    conv = cb_ref[...]
    for w in range(taps):
        off = SUBLANES - (taps - 1) + w
        conv = conv + cw_ref[w:w + 1, :] * xp_scr[off:off + q, :]
    xp_scr[0:SUBLANES, :] = xp_scr[q:q + SUBLANES, :]
    xc = _silu(conv)
    xs = xc[:, :width]
    bm = xc[:, width:width + n_groups * ns]
    cm = xc[:, width + n_groups * ns:]

    row = lax.broadcasted_iota(jnp.int32, (q, q), 0)
    col = lax.broadcasted_iota(jnp.int32, (q, q), 1)
    tri = row >= col
    dt = _softplus(dt_ref[...] + dtb_ref[...])
    dtt = _softplus(dtt_ref[...] + dtbt_ref[...])
    adt = dt * -jnp.exp(alog_ref[...])
    adtt = dtt * -jnp.exp(alogt_ref[...])
    cs = _dot(tri.astype(F32), adt, hi)
    cst = _dot(adtt, (row <= col).astype(F32), hi)
    cs_last = cs[q - 1:q, :]
    e_cs = jnp.exp(cs)
    dec = jnp.exp(cs_last - cs)
    e_last = jnp.exp(cs_last)

    def expand(v):
        hi_ = v.astype(BF16)
        r1 = v - hi_.astype(F32)
        mid = r1.astype(BF16)
        lo = (r1 - mid.astype(F32)).astype(BF16)
        return _dot(hi_, ex_ref[...]) + _dot(mid, ex_ref[...]) + _dot(lo, ex_ref[...])

    dt_x, ecs_x, dec_x = expand(dt), expand(e_cs), expand(dec)
    xdt_all = xs * dt_x
    xdt16 = xdt_all.astype(BF16)
    xdd16 = (xdt_all * dec_x).astype(BF16)
    for g in range(n_groups):
        bg = bm[:, g * ns:(g + 1) * ns].astype(BF16)
        cg = cm[:, g * ns:(g + 1) * ns].astype(BF16)
        cb = lax.dot_general(cg, bg, _NT, preferred_element_type=F32)
        for r in range(rep):
            h = g * rep + r
            cols = slice(h * hp, (h + 1) * hp)
            seg = cs[:, h:h + 1] - cst[h:h + 1, :]
            lm = jnp.exp(jnp.where(tri, seg, NEG_BIG))
            y_scr[:, cols] = _dot((cb * lm).astype(BF16), xdt16[:, cols])
            s_prev = st_scr[h]
            yoff_scr[:, cols] = lax.dot_general(cg, s_prev.astype(BF16), _NT, preferred_element_type=F32)
            st_scr[h] = e_last[:, h:h + 1] * s_prev + lax.dot_general(xdd16[:, cols], bg, _TN, preferred_element_type=F32)

    yg = (y_scr[...] + yoff_scr[...] * ecs_x + dsk_ref[...] * xs) * _silu(z_ref[...])
    o = yg * lax.rsqrt(jnp.mean(yg * yg, axis=-1, keepdims=True) + RMS_EPS) * nw_ref[...]
    o_ref[...] = o.astype(BF16)

    @pl.when(c == pl.num_programs(1) - 1)
    def _():
        hf_ref[...] = st_scr[...]


def _ssd(xbc, z, dt_raw, conv0, h0, params, lyr, n_seq, n_chunks):
    conv_w, conv_b, dtb, dtbt, alog, alogt, dsk, nw, n_groups = params
    _, n_heads, hp, ns = h0.shape
    q = SSD_CHUNK
    ch, width = xbc.shape[1], z.shape[1]
    dtt = jnp.transpose(dt_raw[:, :n_heads].reshape(n_seq * n_chunks, q, n_heads), (0, 2, 1))
    head_lanes = (jnp.arange(LANES)[:, None] == jnp.arange(width)[None, :] // hp).astype(BF16)
    rowblk = lambda wd: pl.BlockSpec((q, wd), lambda b, c: (b * n_chunks + c, 0))
    lyr_blk = lambda *shape: pl.BlockSpec((None,) + shape, lambda b, c: (lyr,) + (0,) * len(shape))
    seq_blk = lambda *shape: pl.BlockSpec((None,) + shape, lambda b, c: (b,) + (0,) * len(shape))
    return pl.pallas_call(
        functools.partial(_ssd_kernel, n_heads=n_heads, hp=hp, n_groups=n_groups, ns=ns),
        grid=(n_seq, n_chunks),
        in_specs=[rowblk(ch), rowblk(width), rowblk(LANES),
                  pl.BlockSpec((None, n_heads, q), lambda b, c: (b * n_chunks + c, 0, 0)),
                  seq_blk(SUBLANES, ch), seq_blk(n_heads, hp, ns),
                  lyr_blk(conv_w.shape[1], ch), lyr_blk(1, ch), lyr_blk(1, LANES), lyr_blk(n_heads, 1),
                  lyr_blk(1, LANES), lyr_blk(n_heads, 1), lyr_blk(1, width), lyr_blk(1, width),
                  pl.BlockSpec((LANES, width), lambda b, c: (0, 0))],
        out_specs=[rowblk(width), seq_blk(n_heads, hp, ns)],
        out_shape=[jax.ShapeDtypeStruct((n_seq * n_chunks * q, width), BF16),
                   jax.ShapeDtypeStruct((n_seq, n_heads, hp, ns), F32)],
        scratch_shapes=[pltpu.VMEM((q + SUBLANES, ch), F32), pltpu.VMEM((n_heads, hp, ns), F32),
                        pltpu.VMEM((q, width), F32), pltpu.VMEM((q, width), F32)],
        compiler_params=_cparams("arbitrary", "arbitrary"),
    )(xbc, z, dt_raw, dtt, conv0, h0, conv_w, conv_b, dtb, dtbt, alog, alogt, dsk, nw, head_lanes)


def _ssd_step_conv_kernel(xbc_ref, hist_ref, dt_ref, cw_ref, cb_ref, dtb_ref, alog_ref, xc_ref, dto_ref, da_ref):
    taps = cw_ref.shape[0]
    conv = cb_ref[...] + cw_ref[taps - 1:taps, :] * xbc_ref[...]
    for w in range(taps - 1):
        conv = conv + cw_ref[w:w + 1, :] * hist_ref[w]
    xc_ref[...] = _silu(conv)
    dt = _softplus(dt_ref[...] + dtb_ref[...])
    dto_ref[...] = dt
    da_ref[...] = jnp.exp(dt * -jnp.exp(alog_ref[...]))


def _ssd_step_kernel(xt_ref, zt_ref, bm_ref, cm_ref, dt_ref, da_ref, s0_ref, dsk_ref, nwt_ref, ot_ref, sn_ref,
                     *, n_heads, rep):
    xt = xt_ref[...]
    xdt = xt * dt_ref[...]
    lane = lax.broadcasted_iota(jnp.int32, xt.shape, 1)
    y = jnp.zeros(xt.shape, F32)
    for h in range(n_heads):
        g = h // rep
        s_new = da_ref[:, h:h + 1] * s0_ref[h] + xdt[:, h:h + 1] * bm_ref[g:g + 1, :]
        sn_ref[h] = s_new
        y = jnp.where(lane == h, jnp.sum(s_new * cm_ref[g:g + 1, :], axis=1, keepdims=True), y)
    yg = (y + dsk_ref[...] * xt) * _silu(zt_ref[...])
    ms = jnp.sum(jnp.sum(yg * yg, axis=1, keepdims=True), axis=0, keepdims=True) / yg.size
    ot_ref[...] = yg * lax.rsqrt(ms + RMS_EPS) * nwt_ref[...]


def _ssd_sample(xbc, z, dt_raw, conv_hist, s0, params, nw_t, lyr):
    conv_w, conv_b, dtb, _, alog, _, dsk, _, n_groups = params
    nb, n_heads, hp, ns = s0.shape
    ch, width = xbc.shape[1], z.shape[1]
    taps = conv_w.shape[1]
    full = lambda shape: pl.BlockSpec(shape, lambda i: (0,) * len(shape))
    lyr1 = lambda *shape: pl.BlockSpec((None,) + shape, lambda i: (lyr,) + (0,) * len(shape))
    xc, dt, da = pl.pallas_call(
        _ssd_step_conv_kernel,
        grid=(1,),
        in_specs=[full((nb, ch)), full((taps - 1, nb, ch)), full((nb, LANES)),
                  lyr1(taps, ch), lyr1(1, ch), lyr1(1, LANES), lyr1(1, LANES)],
        out_specs=[full((nb, ch)), full((nb, LANES)), full((nb, LANES))],
        out_shape=[jax.ShapeDtypeStruct((nb, ch), F32), jax.ShapeDtypeStruct((nb, LANES), F32),
                   jax.ShapeDtypeStruct((nb, LANES), F32)],
        compiler_params=_cparams("arbitrary"),
    )(xbc, jnp.transpose(conv_hist, (1, 0, 2)), dt_raw, conv_w, conv_b, dtb, alog)
    to_t = lambda v: jnp.transpose(v.reshape(nb, n_heads, hp), (0, 2, 1))
    gn = n_groups * ns
    seq = lambda *shape: pl.BlockSpec((None,) + shape, lambda b: (b,) + (0,) * len(shape))
    lyr_b = lambda *shape: pl.BlockSpec((None,) + shape, lambda b: (lyr,) + (0,) * len(shape))
    o_t, s_new = pl.pallas_call(
        functools.partial(_ssd_step_kernel, n_heads=n_heads, rep=n_heads // n_groups),
        grid=(nb,),
        in_specs=[seq(hp, n_heads), seq(hp, n_heads), seq(n_groups, ns), seq(n_groups, ns),
                  seq(1, n_heads), seq(1, n_heads), seq(n_heads, hp, ns), lyr_b(1, n_heads), lyr_b(hp, n_heads)],
        out_specs=[seq(hp, n_heads), seq(n_heads, hp, ns)],
        out_shape=[jax.ShapeDtypeStruct((nb, hp, n_heads), F32), jax.ShapeDtypeStruct((nb, n_heads, hp, ns), F32)],
        compiler_params=_cparams("arbitrary"),
    )(to_t(xc[:, :width]), to_t(z), xc[:, width:width + gn].reshape(nb, n_groups, ns),
      xc[:, width + gn:].reshape(nb, n_groups, ns), dt[:, None, :n_heads], da[:, None, :n_heads], s0,
      dsk[:, :, ::hp], nw_t)
    o = jnp.transpose(o_t, (0, 2, 1)).reshape(nb, width).astype(BF16)
    return o, s_new


def _pad_rows_per_seq(a, rows):
    n_seq, c = a.shape
    return jnp.pad(a[:, None, :], ((0, 0), (0, rows - 1), (0, 0))).reshape(n_seq * rows, c)


def _trunk(xp, xs, mods_p, mods_s, wts, dims, states):
    (w1, w3, w2, w_main, w_dt, w_gates, w_branch, w_out, w_glu, ln_g, ln_b, s5_params, s5_d,
     attn_sink, bias_p, bias_s, ssd_params, nw_t) = wts
    depth, batch, seq, dec_b = dims["depth"], dims["batch"], dims["seq"], dims["dec_b"]
    n_kv, hd, wn = dims["n_kv"], dims["hd"], dims["window"]
    s5_w, ssd_w, taps, ch = dims["s5_w"], dims["ssd_w"], dims["taps"], dims["conv_ch"]
    ssd_heads, hp, ns = dims["ssd_heads"], dims["ssd_hp"], dims["ssd_state"]
    alpha = (2 * depth) ** 0.25
    kw = n_kv * hd
    qw = attn_sink.shape[1] * hd
    o_q = s5_w
    o_z = o_q + qw + 2 * kw
    o_xbc = o_z + ssd_w
    sq = SAMPLE_Q_ROWS

    p_out = {k: [] for k in ("s5_re", "s5_im", "ssm", "conv", "k", "v")}
    s_out = {k: [] for k in p_out}
    x = (xp, xs)
    mods = (mods_p, mods_s)
    h = tuple(_modulate(x[i], mods[i], 0) for i in range(2))

    def norm(x, y, ls, next_ls, coef):
        res = [_deepnorm(x[i], y[i], mods[i], ls, ln_g, ln_b, next_ls, alpha, coef) for i in range(2)]
        return (res[0][0], res[1][0]), (res[0][1], res[1][1])

    for lyr in range(depth):
        ls = 3 * lyr
        y = _mm(*_ffn_up(*h, w1, w3, 2 * lyr), w2, 2 * lyr)
        x, h = norm(x, y, ls, ls + 1, 0.5)

        proj = lambda off, n: _mm(*h, w_main, lyr, col_off=off, n=n)
        u = proj(0, s5_w)
        qkv = proj(o_q, qw + 2 * kw)
        z = proj(o_z, ssd_w)
        xbc = proj(o_xbc, ch)
        dt_raw = _mm(*h, w_dt, lyr)
        gates = _mm(*h, w_gates, lyr, act="sigmoid")

        a_p, re_p, im_p = _s5_prompt(u[0], _s5_prompt_operators(s5_params(lyr), s5_d[lyr]), batch, seq)
        a_s, re_s, im_s = _s5_sample(u[1], _s5_sample_operators(s5_params(lyr), s5_d[lyr]),
                                     states["s5_re"][lyr], states["s5_im"][lyr])
        for out, re, im in ((p_out, re_p, im_p), (s_out, re_s, im_s)):
            out["s5_re"].append(re)
            out["s5_im"].append(im)
        o_a = (_s5_glu(a_p, w_glu, lyr), _s5_glu(a_s, w_glu, lyr))

        n_blk = seq // wn
        cur = lambda b, n: b * n_blk + n
        prev = lambda b, n: b * n_blk + jnp.maximum(n - 1, 0)
        k_col, v_col = qw // kw, qw // kw + 1
        qkv_p, qkv_s = qkv
        ob_p = _swa(qkv_p, (qkv_p, prev, k_col), (qkv_p, cur, k_col), (qkv_p, prev, v_col), (qkv_p, cur, v_col),
                    bias_p, attn_sink[lyr], batch, n_blk, wn, n_kv, hd, first_has_prev=False)
        kv3 = qkv_p.reshape(batch, seq, qw + 2 * kw)
        p_out["k"].append(kv3[:, seq - wn:, qw:qw + kw].reshape(batch, wn, n_kv, hd))
        p_out["v"].append(kv3[:, seq - wn:, qw + kw:].reshape(batch, wn, n_kv, hd))
        one = lambda b, n: b
        k_new, v_new = qkv_s[:, qw:qw + kw], qkv_s[:, qw + kw:]
        k_cache = states["k"][lyr].reshape(dec_b * wn, kw)
        v_cache = states["v"][lyr].reshape(dec_b * wn, kw)
        ob_s = _swa(_pad_rows_per_seq(qkv_s[:, :qw], sq), (k_cache, one, 0), (_pad_rows_per_seq(k_new, wn), one, 0),
                    (v_cache, one, 0), (_pad_rows_per_seq(v_new, wn), one, 0),
                    bias_s, attn_sink[lyr], dec_b, 1, wn, n_kv, hd, first_has_prev=True)
        ob_s = ob_s.reshape(dec_b, sq, qw)[:, 0]
        s_out["k"].append(jnp.concatenate([states["k"][lyr][:, 1:], k_new.reshape(dec_b, 1, n_kv, hd)], axis=1))
        s_out["v"].append(jnp.concatenate([states["v"][lyr][:, 1:], v_new.reshape(dec_b, 1, n_kv, hd)], axis=1))

        cq = SSD_CHUNK
        oc_p, ssm_p = _ssd(xbc[0], z[0], dt_raw[0], jnp.zeros((batch, SUBLANES, ch), F32),
                           jnp.zeros((batch, ssd_heads, hp, ns), F32), ssd_params, lyr, batch, seq // cq)
        p_out["conv"].append(xbc[0].reshape(batch, seq, ch)[:, seq - (taps - 1):])
        conv_hist = states["conv"][lyr]
        oc_s, ssm_s = _ssd_sample(xbc[1], z[1], dt_raw[1], conv_hist, states["ssm"][lyr], ssd_params, nw_t, lyr)
        s_out["conv"].append(jnp.concatenate([conv_hist[:, 1:], xbc[1][:, None, :]], axis=1))
        p_out["ssm"].append(ssm_p)
        s_out["ssm"].append(ssm_s)

        merged = _merge((o_a[0], ob_p, oc_p), (o_a[1], ob_s, oc_s), w_branch, lyr, *gates)
        y = _mm(*merged, w_out, lyr, tn_cap=256)
        x, h = norm(x, y, ls + 1, ls + 2, 1.0)

        y = _mm(*_ffn_up(*h, w1, w3, 2 * lyr + 1), w2, 2 * lyr + 1)
        x, h = norm(x, y, ls + 2, ls + 3 if lyr + 1 < depth else None, 0.5)

    names = ("s5_re", "s5_im", "ssm", "conv", "k", "v")
    return x, tuple(jnp.stack(p_out[k]) for k in names), tuple(jnp.stack(s_out[k]) for k in names)


def kernel(x_prompt, x_sample, state_s5_re, state_s5_im, state_ssm, state_conv, cache_k_win, cache_v_win,
           c_prompt, c_sample, w_ada, b_ada, ln_g, ln_b, ffn_w1, ffn_w3, ffn_w2, w_in, w_branch, w_out,
           s5_lam_re, s5_lam_im, s5_log_dt, s5_b_re, s5_b_im, s5_c_re, s5_c_im, s5_d, s5_w_glu,
           attn_sink, rel_bias, conv_w, conv_b, dt_bias, a_log, ssd_d, ssd_norm_w):
    batch, seq, d = x_prompt.shape
    dec_b, dec_seq, _ = x_sample.shape
    assert dec_seq == 1, "the sample group decodes one token per sequence"
    depth, n_sub = w_ada.shape[:2]
    d_ff = ffn_w1.shape[-1]
    g5, p5, i5 = s5_b_re.shape[1:]
    s5_w = g5 * i5
    n_heads = attn_sink.shape[1]
    _, _, wn, n_kv, hd = cache_k_win.shape
    qw, kw = n_heads * hd, n_kv * hd
    _, _, ssd_heads, hp, ns = state_ssm.shape
    ssd_w = ssd_heads * hp
    taps, ch = conv_w.shape[1:]
    n_groups = (ch - ssd_w) // (2 * ns)
    assert seq % wn == 0 and seq % SSD_CHUNK == 0 and seq % S5_CHUNK == 0
    assert ssd_heads <= LANES and taps - 1 <= SUBLANES and LANES % i5 == 0 and s5_w % LANES == 0

    bf = lambda w: w.astype(BF16)
    w1 = ffn_w1.reshape(depth * 2, d, d_ff)
    w3 = ffn_w3.reshape(depth * 2, d, d_ff)
    w2 = bf(ffn_w2).reshape(depth * 2, d_ff, d)
    o_dt = s5_w + qw + 2 * kw + ssd_w + ch
    o_g = o_dt + ssd_heads
    w_in_b = bf(w_in)
    w_main = w_in_b[:, :, :o_dt]
    w_dt = jnp.pad(w_in_b[:, :, o_dt:o_g], ((0, 0), (0, 0), (0, LANES - ssd_heads)))
    w_gates = w_in_b[:, :, o_g:]

    lane_pad = lambda v: jnp.pad(v, ((0, 0), (0, LANES - v.shape[1])))[:, None, :]
    ssd_params = (conv_w, conv_b[:, None, :], lane_pad(dt_bias), dt_bias[:, :, None],
                  lane_pad(a_log), a_log[:, :, None], jnp.repeat(ssd_d, hp, axis=1)[:, None, :],
                  ssd_norm_w[:, None, :], n_groups)
    s5_params = lambda lyr: (s5_lam_re[lyr], s5_lam_im[lyr], s5_log_dt[lyr], s5_b_re[lyr], s5_b_im[lyr],
                             s5_c_re[lyr], s5_c_im[lyr])
    bias_p = _rel_bias_table(rel_bias, wn, n_kv)
    rep = n_heads // n_kv
    bias_s = bias_p.reshape(n_kv, rep, wn, 2 * wn)[:, :, :SAMPLE_Q_ROWS].reshape(n_kv, rep * SAMPLE_Q_ROWS, 2 * wn)
    wts = (w1, w3, w2, w_main, w_dt, w_gates, bf(w_branch), w_out, bf(s5_w_glu),
           ln_g.reshape(depth * n_sub, 1, d), ln_b.reshape(depth * n_sub, 1, d), s5_params, s5_d,
           attn_sink, bias_p, bias_s, ssd_params,
           jnp.transpose(ssd_norm_w.reshape(depth, ssd_heads, hp), (0, 2, 1)))

    rows = -(-(dec_b + batch) // SUBLANES) * SUBLANES
    c_all = jnp.pad(jnp.concatenate([c_sample, c_prompt], axis=0), ((0, rows - dec_b - batch), (0, 0)))
    mods = _adaln(c_all, w_ada.reshape(depth * n_sub, d, 3 * d), b_ada.reshape(depth * n_sub, 1, 3 * d))

    dims = dict(depth=depth, batch=batch, seq=seq, dec_b=dec_b, n_kv=n_kv, hd=hd, window=wn, s5_w=s5_w,
                ssd_w=ssd_w, ssd_heads=ssd_heads, ssd_hp=hp, ssd_state=ns, taps=taps, conv_ch=ch)
    states = dict(s5_re=state_s5_re, s5_im=state_s5_im, ssm=state_ssm, conv=state_conv, k=cache_k_win, v=cache_v_win)
    (yp, ys), p_state, s_state = _trunk(x_prompt.reshape(batch * seq, d), x_sample.reshape(dec_b, d),
                                        _Mods(mods, True, dec_b, seq, d), _Mods(mods, False, dec_b, 1, d),
                                        wts, dims, states)
    return (yp.reshape(batch, seq, d), ys.reshape(dec_b, 1, d)) + p_state + s_state
```
